```python
import math
import jax, jax.numpy as jnp
from jax import lax
import numpy as np

D_MODEL = 1024
BATCH = 4
SEQ = 8192
DEPTH = 1

N_META = 16
MIX_WIDTH = D_MODEL
DIFF_WIDTH = MIX_WIDTH // 2
WIN_WIDTH = MIX_WIDTH - DIFF_WIDTH
DIFF_HEAD_DIM = 64
DIFF_HEADS = DIFF_WIDTH // (2 * DIFF_HEAD_DIM)
WIN_HEAD_DIM = 64
WIN_HEADS = WIN_WIDTH // WIN_HEAD_DIM
WIN_KV_HEADS = 2
WIN_GROUP = WIN_HEADS // WIN_KV_HEADS
WINDOW = 128
BLOCK = 128
D_FF = ((8 * D_MODEL // 3 + 127) // 128) * 128
ROPE_THETA = 10000.0
EPS = 1e-6
NEG = -1e30

A_Q = DIFF_HEADS * 2 * DIFF_HEAD_DIM
A_K = DIFF_HEADS * 2 * DIFF_HEAD_DIM
A_V = DIFF_HEADS * 2 * DIFF_HEAD_DIM
B_Q = WIN_HEADS * WIN_HEAD_DIM
B_K = WIN_KV_HEADS * WIN_HEAD_DIM
B_V = WIN_KV_HEADS * WIN_HEAD_DIM
IN_WIDTH = A_Q + A_K + A_V + B_Q + B_K + B_V

kernel_name = "hymba_diff_window_macaron_encoder"


def rmsnorm(x, g):
    xf = x.astype(jnp.float32)
    y = xf * lax.rsqrt(jnp.mean(xf * xf, axis=-1, keepdims=True) + EPS)
    return (y * g.astype(jnp.float32)).astype(x.dtype)


def swiglu(x, w_gate, w_up, w_down):
    return (jax.nn.silu(x @ w_gate) * (x @ w_up)) @ w_down


def rope_tables(length, dim):
    pos = jnp.arange(length, dtype=jnp.float32)
    inv = ROPE_THETA ** (-jnp.arange(0, dim, 2, dtype=jnp.float32) / dim)
    ang = pos[:, None] * inv[None, :]
    return jnp.cos(ang), jnp.sin(ang)


def apply_rope(x, cos, sin):
    shape = (cos.shape[0],) + (1,) * (x.ndim - 3) + (cos.shape[1],)
    c = cos.reshape(shape).astype(x.dtype)
    s = sin.reshape(shape).astype(x.dtype)
    half = x.shape[-1] // 2
    x1, x2 = x[..., :half], x[..., half:]
    return jnp.concatenate([x1 * c - x2 * s, x2 * c + x1 * s], axis=-1)


def diff_attention(q, k, v, lam, sub_gain, lambda_init):
    b_, l_ = q.shape[0], q.shape[1]
    scale = DIFF_HEAD_DIM ** -0.5

    def attend(qb):
        s = jnp.einsum('bqhcd,bkhcd->bhcqk', qb, k, preferred_element_type=jnp.float32) * scale
        p = jax.nn.softmax(s, axis=-1)
        a = p[:, :, 0] - lam * p[:, :, 1]
        return jnp.einsum('bhqk,bkhe->bqhe', a.astype(v.dtype), v)

    out_meta = attend(q[:, :N_META])
    nb = (l_ - N_META) // BLOCK
    qr = q[:, N_META:].reshape(b_, nb, BLOCK, DIFF_HEADS, 2, DIFF_HEAD_DIM).transpose(1, 0, 2, 3, 4, 5)
    out_real = lax.map(attend, qr)
    out_real = out_real.transpose(1, 0, 2, 3, 4).reshape(b_, nb * BLOCK, DIFF_HEADS, 2 * DIFF_HEAD_DIM)
    o = jnp.concatenate([out_meta, out_real], axis=1)
    o = rmsnorm(o, sub_gain) * (1.0 - lambda_init)
    return o.reshape(b_, l_, DIFF_WIDTH)


def window_attention(q, k, v, sink):
    b_, l_ = q.shape[0], q.shape[1]
    s_len = l_ - N_META
    nb = s_len // BLOCK
    scale = WIN_HEAD_DIM ** -0.5
    sink_f = sink.astype(jnp.float32)[:, :, None, None]

    def softmax_with_sink(s):
        sk = jnp.broadcast_to(sink_f, s.shape[:-1] + (1,))
        return jax.nn.softmax(jnp.concatenate([s, sk], axis=-1), axis=-1)[..., :-1]

    s_m = jnp.einsum('bqhgd,bkhd->bhgqk', q[:, :N_META], k, preferred_element_type=jnp.float32) * scale
    o_m = jnp.einsum('bhgqk,bkhd->bqhgd', softmax_with_sink(s_m).astype(v.dtype), v)

    km, vm = k[:, :N_META], v[:, :N_META]
    qr = q[:, N_META:].reshape(b_, nb, BLOCK, WIN_KV_HEADS, WIN_GROUP, WIN_HEAD_DIM)

    def neighbours(t):
        tb = t.reshape(b_, nb, BLOCK, WIN_KV_HEADS, WIN_HEAD_DIM)
        tp = jnp.pad(tb, ((0, 0), (1, 1), (0, 0), (0, 0), (0, 0)))
        return jnp.concatenate([tp[:, :-2], tp[:, 1:-1], tp[:, 2:]], axis=2)

    kr = neighbours(k[:, N_META:])
    vr = neighbours(v[:, N_META:])
    s_meta = jnp.einsum('bnqhgd,bkhd->bnhgqk', qr, km, preferred_element_type=jnp.float32) * scale
    s_band = jnp.einsum('bnqhgd,bnkhd->bnhgqk', qr, kr, preferred_element_type=jnp.float32) * scale
    qi = jnp.arange(BLOCK)[:, None]
    kj = jnp.arange(3 * BLOCK)[None, :] - BLOCK
    kabs = jnp.arange(nb)[:, None, None] * BLOCK + kj[None]
    valid = (jnp.abs(kj - qi)[None] <= WINDOW) & (kabs >= 0) & (kabs < s_len)
    s_band = jnp.where(valid[None, :, None, None], s_band, NEG)
    p = softmax_with_sink(jnp.concatenate([s_meta, s_band], axis=-1)).astype(v.dtype)
    o_r = (jnp.einsum('bnhgqk,bkhd->bnqhgd', p[..., :N_META], vm)
           + jnp.einsum('bnhgqk,bnkhd->bnqhgd', p[..., N_META:], vr))
    o_r = o_r.reshape(b_, s_len, WIN_KV_HEADS, WIN_GROUP, WIN_HEAD_DIM)
    o = jnp.concatenate([o_m, o_r], axis=1)
    return o.reshape(b_, l_, WIN_WIDTH)


def setup_inputs(seed: int = 0) -> dict:
    key = jax.random.key(seed)
    ks = jax.random.split(key, 24)
    f32 = jnp.float32

    def nrm(k, shape, scale):
        return jax.random.normal(k, shape, f32) * scale

    def gain(k, shape):
        return 1.0 + 0.02 * jax.random.normal(k, shape, f32)

    return {
        "x": nrm(ks[0], (BATCH, SEQ, D_MODEL), 1.0),
        "meta_tokens": nrm(ks[1], (N_META, D_MODEL), 1.0),
        "ffn1_norm": gain(ks[2], (DEPTH, D_MODEL)),
        "ffn1_w_gate": nrm(ks[3], (DEPTH, D_MODEL, D_FF), D_MODEL ** -0.5),
        "ffn1_w_up": nrm(ks[4], (DEPTH, D_MODEL, D_FF), D_MODEL ** -0.5),
        "ffn1_w_down": nrm(ks[5], (DEPTH, D_FF, D_MODEL), D_FF ** -0.5),
        "mix_norm": gain(ks[6], (DEPTH, D_MODEL)),
        "w_in": nrm(ks[7], (DEPTH, D_MODEL, IN_WIDTH), D_MODEL ** -0.5),
        "lambda_q1": nrm(ks[8], (DEPTH, DIFF_HEAD_DIM), 0.1),
        "lambda_k1": nrm(ks[9], (DEPTH, DIFF_HEAD_DIM), 0.1),
        "lambda_q2": nrm(ks[10], (DEPTH, DIFF_HEAD_DIM), 0.1),
        "lambda_k2": nrm(ks[11], (DEPTH, DIFF_HEAD_DIM), 0.1),
        "diff_norm": gain(ks[12], (DEPTH, 2 * DIFF_HEAD_DIM)),
        "win_sink": nrm(ks[13], (DEPTH, WIN_HEADS), 0.5),
        "win_norm": gain(ks[14], (DEPTH, WIN_WIDTH)),
        "w_out": nrm(ks[15], (DEPTH, MIX_WIDTH, D_MODEL), MIX_WIDTH ** -0.5),
        "ffn2_norm": gain(ks[16], (DEPTH, D_MODEL)),
        "ffn2_w_gate": nrm(ks[17], (DEPTH, D_MODEL, D_FF), D_MODEL ** -0.5),
        "ffn2_w_up": nrm(ks[18], (DEPTH, D_MODEL, D_FF), D_MODEL ** -0.5),
        "ffn2_w_down": nrm(ks[19], (DEPTH, D_FF, D_MODEL), D_FF ** -0.5),
        "final_norm": gain(ks[20], (D_MODEL,)),
    }


def reference(x, meta_tokens, ffn1_norm, ffn1_w_gate, ffn1_w_up, ffn1_w_down, mix_norm, w_in,
              lambda_q1, lambda_k1, lambda_q2, lambda_k2, diff_norm, win_sink, win_norm, w_out,
              ffn2_norm, ffn2_w_gate, ffn2_w_up, ffn2_w_down, final_norm):
    b_ = x.shape[0]
    meta = jnp.broadcast_to(meta_tokens.astype(x.dtype)[None], (b_, N_META, D_MODEL))
    h = jnp.concatenate([meta, x], axis=1)
    l_ = h.shape[1]
    cos, sin = rope_tables(l_, DIFF_HEAD_DIM)

    for l in range(DEPTH):
        h = h + 0.5 * swiglu(rmsnorm(h, ffn1_norm[l]), ffn1_w_gate[l], ffn1_w_up[l], ffn1_w_down[l])

        u = rmsnorm(h, mix_norm[l])
        z = u @ w_in[l]
        o0 = 0
        qa = z[..., o0:o0 + A_Q].reshape(b_, l_, DIFF_HEADS, 2, DIFF_HEAD_DIM); o0 += A_Q
        ka = z[..., o0:o0 + A_K].reshape(b_, l_, DIFF_HEADS, 2, DIFF_HEAD_DIM); o0 += A_K
        va = z[..., o0:o0 + A_V].reshape(b_, l_, DIFF_HEADS, 2 * DIFF_HEAD_DIM); o0 += A_V
        qb = z[..., o0:o0 + B_Q].reshape(b_, l_, WIN_KV_HEADS, WIN_GROUP, WIN_HEAD_DIM); o0 += B_Q
        kb = z[..., o0:o0 + B_K].reshape(b_, l_, WIN_KV_HEADS, WIN_HEAD_DIM); o0 += B_K
        vb = z[..., o0:o0 + B_V].reshape(b_, l_, WIN_KV_HEADS, WIN_HEAD_DIM)

        qa = apply_rope(qa, cos, sin)
        ka = apply_rope(ka, cos, sin)
        lambda_init = 0.8 - 0.6 * math.exp(-0.3 * l)
        lam = (jnp.exp(jnp.sum(lambda_q1[l].astype(jnp.float32) * lambda_k1[l].astype(jnp.float32)))
               - jnp.exp(jnp.sum(lambda_q2[l].astype(jnp.float32) * lambda_k2[l].astype(jnp.float32)))
               + lambda_init)
        out_a = diff_attention(qa, ka, va, lam, diff_norm[l], lambda_init)

        qb = apply_rope(qb, cos, sin)
        kb = apply_rope(kb, cos, sin)
        out_b = window_attention(qb, kb, vb, win_sink[l].reshape(WIN_KV_HEADS, WIN_GROUP))
        out_b = rmsnorm(out_b, win_norm[l])

        h = h + jnp.concatenate([out_a, out_b], axis=-1) @ w_out[l]

        h = h + 0.5 * swiglu(rmsnorm(h, ffn2_norm[l]), ffn2_w_gate[l], ffn2_w_up[l], ffn2_w_down[l])

    h = rmsnorm(h, final_norm)
    return h[:, N_META:]
```

```python
import functools
import math

import jax
import jax.numpy as jnp
from jax import lax
from jax.experimental import pallas as pl
from jax.experimental.pallas import tpu as pltpu

N_META = 16
HEAD_DIM = 64
DIFF_HEADS = 4
WIN_KV_HEADS = 2
WIN_GROUP = 4
WINDOW = 128
ROPE_THETA = 10000.0
EPS = 1e-6
NEG = -1e30
LAMBDA_INIT = 0.8 - 0.6 * math.exp(-0.3 * 0)
QK_SCALE = HEAD_DIM ** -0.5

LANES = 128
VMEM_LIMIT_ROWWISE = 56 * 1024 * 1024
VMEM_LIMIT_ATTN = 40 * 1024 * 1024

ROW_TILE = 512
FF_CHUNK = 1408
DIFF_TQ = 512
DIFF_TK = 512
WIN_TQ = 256
WIN_SPAN = WIN_TQ + 2 * WINDOW

F32 = jnp.float32
BF16 = jnp.bfloat16
_NT = (((1,), (1,)), ((), ()))


def _rms(x, gain):
    return x * lax.rsqrt(jnp.mean(x * x, axis=-1, keepdims=True) + EPS) * gain


def _swiglu(xn, wg_ref, wu_ref, wd_ref):
    d_ff = wg_ref.shape[1]
    acc = None
    for c in range(d_ff // FF_CHUNK):
        sl = slice(c * FF_CHUNK, (c + 1) * FF_CHUNK)
        g = jnp.dot(xn, wg_ref[:, sl], preferred_element_type=F32)
        u = jnp.dot(xn, wu_ref[:, sl], preferred_element_type=F32)
        a = (g * jax.nn.sigmoid(g) * u).astype(BF16)
        d = jnp.dot(a, wd_ref[sl, :], preferred_element_type=F32)
        acc = d if acc is None else acc + d
    return acc


def _rope_block(x, cos, sin_signed, first_half):
    partner = jnp.where(first_half, pltpu.roll(x, LANES - 32, 1), pltpu.roll(x, 32, 1))
    return x * cos + partner * sin_signed


def _ffn_in_kernel(x_ref, cos_ref, sin_ref, n1_ref, wg_ref, wu_ref, wd_ref, n2_ref, win_ref,
                   h_ref, qa_ref, ka_ref, va_ref, qb_ref, kb_ref, vb_ref):
    x = x_ref[...]
    xn = _rms(x, n1_ref[...]).astype(BF16)
    h = x + 0.5 * _swiglu(xn, wg_ref, wu_ref, wd_ref)
    h_ref[...] = h
    u = _rms(h, n2_ref[...]).astype(BF16)

    rows = x.shape[0]
    cos = cos_ref[...]
    sin_signed = sin_ref[...]
    lane = lax.broadcasted_iota(jnp.int32, (rows, LANES), 1)
    first_half = (lane % HEAD_DIM) < (HEAD_DIM // 2)
    low = lane < HEAD_DIM

    def project(col0, width):
        return jnp.dot(u, win_ref[:, col0:col0 + width], preferred_element_type=F32)

    def store_rope(dst_ref, z, scale):
        for i in range(z.shape[1] // LANES):
            blk = _rope_block(z[:, i * LANES:(i + 1) * LANES], cos, sin_signed, first_half)
            if scale != 1.0:
                blk = blk * scale
            dst_ref[:, i * LANES:(i + 1) * LANES] = blk.astype(dst_ref.dtype)

    def store_padded(dst_ref, z):
        swapped = pltpu.roll(z, HEAD_DIM, 1)
        zero = jnp.zeros_like(z)
        parts = (jnp.where(low, z, zero), jnp.where(low, zero, swapped),
                 jnp.where(low, swapped, zero), jnp.where(low, zero, z))
        for i, part in enumerate(parts):
            dst_ref[:, i * LANES:(i + 1) * LANES] = part.astype(dst_ref.dtype)

    store_rope(qa_ref, project(0, 512), QK_SCALE)
    store_rope(ka_ref, project(512, 512), 1.0)
    va_ref[...] = project(1024, 512).astype(va_ref.dtype)
    store_rope(qb_ref, project(1536, 512), QK_SCALE)
    zk = _rope_block(project(2048, LANES), cos, sin_signed, first_half)
    store_padded(kb_ref, zk)
    store_padded(vb_ref, project(2176, LANES))


def _const_spec(shape):
    zeros = (0,) * len(shape)
    return pl.BlockSpec(shape, lambda *_: zeros, pipeline_mode=pl.Buffered(1))


def _ffn_in(x, cos, sin_signed, n1, wg, wu, wd, n2, win, row_tile):
    rows, d = x.shape
    pos_tiles = cos.shape[0] // row_tile
    row = lambda i: (i, 0)
    out_w = 512
    return pl.pallas_call(
        _ffn_in_kernel,
        grid=(rows // row_tile,),
        in_specs=[
            pl.BlockSpec((row_tile, d), row),
            pl.BlockSpec((row_tile, LANES), lambda i: (i % pos_tiles, 0)),
            pl.BlockSpec((row_tile, LANES), lambda i: (i % pos_tiles, 0)),
            _const_spec(n1.shape), _const_spec(wg.shape), _const_spec(wu.shape), _const_spec(wd.shape),
            _const_spec(n2.shape), _const_spec(win.shape),
        ],
        out_specs=[pl.BlockSpec((row_tile, d), row)] + [pl.BlockSpec((row_tile, out_w), row)] * 6,
        out_shape=[jax.ShapeDtypeStruct((rows, d), F32)] + [jax.ShapeDtypeStruct((rows, out_w), BF16)] * 6,
        compiler_params=pltpu.CompilerParams(dimension_semantics=("parallel",),
                                             vmem_limit_bytes=VMEM_LIMIT_ROWWISE),
        name="ffn_in",
    )(x, cos, sin_signed, n1, wg, wu, wd, n2, win)


def _diff_attn_kernel(lq1_ref, lk1_ref, lq2_ref, lk2_ref, gain_ref, q_ref, km_ref, vmt_ref, k_ref, vt_ref,
                      o_ref, m_sc, l_sc, acc_sc):
    lam = (jnp.exp(jnp.sum(lq1_ref[...] * lk1_ref[...], keepdims=True))
           - jnp.exp(jnp.sum(lq2_ref[...] * lk2_ref[...], keepdims=True)) + LAMBDA_INIT)
    q = q_ref[...]
    lane = lax.broadcasted_iota(jnp.int32, q.shape, 1)
    zero = jnp.zeros_like(q)
    qz = (jnp.where(lane < HEAD_DIM, q, zero), jnp.where(lane < HEAD_DIM, zero, q))

    def fold(c, kt, vt, first):
        s = lax.dot_general(kt, qz[c], _NT, preferred_element_type=F32)
        s_max = jnp.max(s, axis=0, keepdims=True)
        m_new = s_max if first else jnp.maximum(m_sc[c], s_max)
        p = jnp.exp(s - m_new)
        l_new = jnp.sum(p, axis=0, keepdims=True)
        acc_new = jnp.dot(vt, p.astype(BF16), preferred_element_type=F32)
        if not first:
            alpha = jnp.exp(m_sc[c] - m_new)
            l_new = alpha * l_sc[c] + l_new
            acc_new = alpha * acc_sc[c] + acc_new
        m_sc[c] = m_new
        l_sc[c] = l_new
        acc_sc[c] = acc_new

    for c in range(2):
        fold(c, km_ref[...], vmt_ref[0], True)

    tk = vt_ref.shape[4]

    def body(t, carry):
        kt = k_ref[pl.ds(pl.multiple_of(t * tk, tk), tk), :]
        vt = vt_ref[0, 0, t]
        for c in range(2):
            fold(c, kt, vt, False)
        return carry

    lax.fori_loop(0, vt_ref.shape[2], body, 0)

    a = acc_sc[0] / l_sc[0] - lam * (acc_sc[1] / l_sc[1])
    ms = jnp.mean(a * a, axis=0, keepdims=True)
    y = a * lax.rsqrt(ms + EPS) * gain_ref[...] * (1.0 - LAMBDA_INIT)
    o_ref[...] = y.T.astype(o_ref.dtype)


def _diff_attn(lams, gain_col, qa, ka_meta, vmt, ka, vt_tiles, batch):
    rows = qa.shape[0]
    seq = rows // batch
    nq = seq // DIFF_TQ
    nk, vdim, tk = vt_tiles.shape[2:]
    lam_spec = pl.BlockSpec((1, HEAD_DIM), lambda b, h, i: (0, 0))
    return pl.pallas_call(
        _diff_attn_kernel,
        grid=(batch, DIFF_HEADS, nq),
        in_specs=[
            lam_spec, lam_spec, lam_spec, lam_spec,
            pl.BlockSpec((vdim, 1), lambda b, h, i: (0, 0)),
            pl.BlockSpec((DIFF_TQ, LANES), lambda b, h, i: (b * nq + i, h)),
            pl.BlockSpec((N_META, LANES), lambda b, h, i: (0, h)),
            pl.BlockSpec((1, vdim, N_META), lambda b, h, i: (h, 0, 0)),
            pl.BlockSpec((seq, LANES), lambda b, h, i: (b, h)),
            pl.BlockSpec((1, 1, nk, vdim, tk), lambda b, h, i: (b, h, 0, 0, 0)),
        ],
        out_specs=pl.BlockSpec((DIFF_TQ, LANES), lambda b, h, i: (b * nq + i, h)),
        out_shape=jax.ShapeDtypeStruct((rows, DIFF_HEADS * vdim), BF16),
        scratch_shapes=[pltpu.VMEM((2, 1, DIFF_TQ), F32), pltpu.VMEM((2, 1, DIFF_TQ), F32),
                        pltpu.VMEM((2, vdim, DIFF_TQ), F32)],
        compiler_params=pltpu.CompilerParams(dimension_semantics=("parallel", "parallel", "arbitrary"),
                                             vmem_limit_bytes=VMEM_LIMIT_ATTN),
        name="diff_attn",
    )(*lams, gain_col, qa, ka_meta, vmt, ka, vt_tiles)


def _win_attn_kernel(sink_ref, q_ref, km_ref, vm_ref, k_ref, v_ref, o_ref):
    g = pl.program_id(1)
    i = pl.program_id(2)
    seq = k_ref.shape[0]
    t0 = i * WIN_TQ
    ks = pl.multiple_of(jnp.clip(t0 - WINDOW, 0, seq - WIN_SPAN), WINDOW)
    kt = k_ref[pl.ds(ks, WIN_SPAN), :]
    vt = v_ref[pl.ds(ks, WIN_SPAN), :]
    qpos = t0 + lax.broadcasted_iota(jnp.int32, (WIN_TQ, WIN_SPAN), 0)
    kpos = ks + lax.broadcasted_iota(jnp.int32, (WIN_TQ, WIN_SPAN), 1)
    valid = jnp.abs(kpos - qpos) <= WINDOW
    for pair in range(WIN_GROUP // 2):
        qp = q_ref[:, pair * LANES:(pair + 1) * LANES]
        o_pair = None
        for par in range(2):
            cols = slice(par * LANES, (par + 1) * LANES)
            sink = sink_ref[g * WIN_GROUP + 2 * pair + par]
            s = lax.dot_general(qp, kt[:, cols], _NT, preferred_element_type=F32)
            s = jnp.where(valid, s, NEG)
            sm = lax.dot_general(qp, km_ref[:, cols], _NT, preferred_element_type=F32)
            m = jnp.maximum(jnp.maximum(jnp.max(s, axis=1, keepdims=True),
                                        jnp.max(sm, axis=1, keepdims=True)), sink)
            p = jnp.exp(s - m)
            pm = jnp.exp(sm - m)
            l = jnp.sum(p, axis=1, keepdims=True) + jnp.sum(pm, axis=1, keepdims=True) + jnp.exp(sink - m)
            o = (jnp.dot(p.astype(BF16), vt[:, cols], preferred_element_type=F32)
                 + jnp.dot(pm.astype(BF16), vm_ref[:, cols], preferred_element_type=F32)) / l
            o_pair = o if o_pair is None else o_pair + o
        o_ref[:, pair * LANES:(pair + 1) * LANES] = o_pair.astype(o_ref.dtype)


def _win_attn(sink, qb, kb_meta, vb_meta, kb, vb, batch):
    rows = qb.shape[0]
    seq = rows // batch
    nq = seq // WIN_TQ
    width = 2 * LANES
    return pl.pallas_call(
        _win_attn_kernel,
        grid=(batch, WIN_KV_HEADS, nq),
        in_specs=[
            pl.BlockSpec(memory_space=pltpu.SMEM),
            pl.BlockSpec((WIN_TQ, width), lambda b, g, i: (b * nq + i, g)),
            pl.BlockSpec((N_META, width), lambda b, g, i: (0, g)),
            pl.BlockSpec((N_META, width), lambda b, g, i: (0, g)),
            pl.BlockSpec((seq, width), lambda b, g, i: (b, g)),
            pl.BlockSpec((seq, width), lambda b, g, i: (b, g)),
        ],
        out_specs=pl.BlockSpec((WIN_TQ, width), lambda b, g, i: (b * nq + i, g)),
        out_shape=jax.ShapeDtypeStruct((rows, WIN_KV_HEADS * width), BF16),
        compiler_params=pltpu.CompilerParams(dimension_semantics=("parallel", "parallel", "arbitrary"),
                                             vmem_limit_bytes=VMEM_LIMIT_ATTN),
        name="win_attn",
    )(sink, qb, kb_meta, vb_meta, kb, vb)


def _out_ffn_kernel(h_ref, oa_ref, ob_ref, wn_ref, wo_ref, n_ref, wg_ref, wu_ref, wd_ref, fn_ref, out_ref):
    half = oa_ref.shape[1]
    ob = _rms(ob_ref[...].astype(F32), wn_ref[...]).astype(BF16)
    mix = (jnp.dot(oa_ref[...], wo_ref[:half, :], preferred_element_type=F32)
           + jnp.dot(ob, wo_ref[half:, :], preferred_element_type=F32))
    h = h_ref[...] + mix
    xn = _rms(h, n_ref[...]).astype(BF16)
    h = h + 0.5 * _swiglu(xn, wg_ref, wu_ref, wd_ref)
    out_ref[...] = _rms(h, fn_ref[...])


def _out_ffn(h1, oa, ob, wn, wo, n, wg, wu, wd, fn):
    rows, d = h1.shape
    half = oa.shape[1]
    row = lambda i: (i, 0)
    return pl.pallas_call(
        _out_ffn_kernel,
        grid=(rows // ROW_TILE,),
        in_specs=[
            pl.BlockSpec((ROW_TILE, d), row), pl.BlockSpec((ROW_TILE, half), row), pl.BlockSpec((ROW_TILE, half), row),
            _const_spec(wn.shape), _const_spec(wo.shape), _const_spec(n.shape),
            _const_spec(wg.shape), _const_spec(wu.shape), _const_spec(wd.shape), _const_spec(fn.shape),
        ],
        out_specs=pl.BlockSpec((ROW_TILE, d), row),
        out_shape=jax.ShapeDtypeStruct((rows, d), F32),
        compiler_params=pltpu.CompilerParams(dimension_semantics=("parallel",),
                                             vmem_limit_bytes=VMEM_LIMIT_ROWWISE),
        name="out_ffn",
    )(h1, oa, ob, wn, wo, n, wg, wu, wd, fn)


def _rope_tables(length):
    pos = jnp.arange(length, dtype=F32)
    inv = ROPE_THETA ** (-jnp.arange(0, HEAD_DIM, 2, dtype=F32) / HEAD_DIM)
    ang = pos[:, None] * inv[None, :]
    cos, sin = jnp.cos(ang), jnp.sin(ang)
    return jnp.tile(cos, (1, 4)), jnp.tile(jnp.concatenate([-sin, sin], axis=1), (1, 2))


def kernel(x, meta_tokens, ffn1_norm, ffn1_w_gate, ffn1_w_up, ffn1_w_down, mix_norm, w_in, lambda_q1, lambda_k1, lambda_q2, lambda_k2, diff_norm, win_sink, win_norm, w_out, ffn2_norm, ffn2_w_gate, ffn2_w_up, ffn2_w_down, final_norm):
    batch, seq, d = x.shape
    assert ffn1_norm.shape[0] == 1, "single layer only"
    assert seq % ROW_TILE == 0 and seq % DIFF_TQ == 0 and seq % DIFF_TK == 0 and seq % WIN_TQ == 0

    cos, sin_signed = _rope_tables(N_META + seq)
    ffn_in_weights = (ffn1_norm, ffn1_w_gate[0].astype(BF16), ffn1_w_up[0].astype(BF16),
                      ffn1_w_down[0].astype(BF16), mix_norm, w_in[0].astype(BF16))

    meta = _ffn_in(meta_tokens.astype(x.dtype), cos[:N_META], sin_signed[:N_META], *ffn_in_weights, N_META)
    real = _ffn_in(x.reshape(batch * seq, d), cos[N_META:], sin_signed[N_META:], *ffn_in_weights, ROW_TILE)
    _, _, ka_m, va_m, _, kb_m, vb_m = meta
    h1, qa, ka, va, qb, kb, vb = real

    vdim = 2 * HEAD_DIM
    nk = seq // DIFF_TK
    vt_tiles = va.reshape(batch, nk, DIFF_TK, DIFF_HEADS, vdim).transpose(0, 3, 1, 4, 2)
    vmt = va_m.reshape(N_META, DIFF_HEADS, vdim).transpose(1, 2, 0)
    lams = (lambda_q1, lambda_k1, lambda_q2, lambda_k2)
    out_a = _diff_attn(lams, diff_norm.reshape(vdim, 1), qa, ka_m, vmt, ka, vt_tiles, batch)

    out_b = _win_attn(win_sink.reshape(-1), qb, kb_m, vb_m, kb, vb, batch)

    out = _out_ffn(h1, out_a, out_b, win_norm, w_out[0].astype(BF16), ffn2_norm,
                   ffn2_w_gate[0].astype(BF16), ffn2_w_up[0].astype(BF16), ffn2_w_down[0].astype(BF16),
                   final_norm.reshape(1, d))
    return out.reshape(batch, seq, d)
```

```python
import functools
import math

import jax
import jax.numpy as jnp
from jax import lax
from jax.experimental import pallas as pl
from jax.experimental.pallas import tpu as pltpu

N_META = 16
HEAD_DIM = 64
DIFF_HEADS = 4
WIN_KV_HEADS = 2
WIN_GROUP = 4
WINDOW = 128
ROPE_THETA = 10000.0
EPS = 1e-6
NEG = -1e30
LAMBDA_INIT = 0.8 - 0.6 * math.exp(-0.3 * 0)
QK_SCALE = HEAD_DIM ** -0.5
LOG2E = math.log2(math.e)

LANES = 128
VMEM_LIMIT_ROWWISE = 56 * 1024 * 1024
VMEM_LIMIT_ATTN = 40 * 1024 * 1024

ROW_TILE = 512
FF_CHUNK = 1408
DIFF_TQ = 512
DIFF_TK = 512
WIN_TQ = 256
WIN_SPAN = WIN_TQ + 2 * WINDOW

F32 = jnp.float32
BF16 = jnp.bfloat16
_NT = (((1,), (1,)), ((), ()))


def _rms(x, gain):
    return x * lax.rsqrt(jnp.mean(x * x, axis=-1, keepdims=True) + EPS) * gain


def _swiglu(xn, wg_ref, wu_ref, wd_ref):
    d_ff = wg_ref.shape[1]
    acc = None
    for c in range(d_ff // FF_CHUNK):
        sl = slice(c * FF_CHUNK, (c + 1) * FF_CHUNK)
        g = jnp.dot(xn, wg_ref[:, sl], preferred_element_type=F32)
        u = jnp.dot(xn, wu_ref[:, sl], preferred_element_type=F32)
        a = (g * jax.nn.sigmoid(g) * u).astype(BF16)
        d = jnp.dot(a, wd_ref[sl, :], preferred_element_type=F32)
        acc = d if acc is None else acc + d
    return acc


def _rope_block(x, cos, sin_signed, first_half):
    partner = jnp.where(first_half, pltpu.roll(x, LANES - 32, 1), pltpu.roll(x, 32, 1))
    return x * cos + partner * sin_signed


def _ffn_in_kernel(x_ref, cos_ref, sin_ref, n1_ref, wg_ref, wu_ref, wd_ref, n2_ref, win_ref,
                   h_ref, qa_ref, ka_ref, va_ref, qb_ref, kb_ref, vb_ref):
    x = x_ref[...]
    xn = _rms(x, n1_ref[...]).astype(BF16)
    h = x + 0.5 * _swiglu(xn, wg_ref, wu_ref, wd_ref)
    h_ref[...] = h
    u = _rms(h, n2_ref[...]).astype(BF16)

    rows = x.shape[0]
    cos = cos_ref[...]
    sin_signed = sin_ref[...]
    lane = lax.broadcasted_iota(jnp.int32, (rows, LANES), 1)
    first_half = (lane % HEAD_DIM) < (HEAD_DIM // 2)
    low = lane < HEAD_DIM

    def project(col0, width):
        return jnp.dot(u, win_ref[:, col0:col0 + width], preferred_element_type=F32)

    def store_rope(dst_ref, z, scale):
        for i in range(z.shape[1] // LANES):
            blk = _rope_block(z[:, i * LANES:(i + 1) * LANES], cos, sin_signed, first_half)
            if scale != 1.0:
                blk = blk * scale
            dst_ref[:, i * LANES:(i + 1) * LANES] = blk.astype(dst_ref.dtype)

    def store_padded(dst_ref, z):
        swapped = pltpu.roll(z, HEAD_DIM, 1)
        zero = jnp.zeros_like(z)
        parts = (jnp.where(low, z, zero), jnp.where(low, zero, swapped),
                 jnp.where(low, swapped, zero), jnp.where(low, zero, z))
        for i, part in enumerate(parts):
            dst_ref[:, i * LANES:(i + 1) * LANES] = part.astype(dst_ref.dtype)

    store_rope(qa_ref, project(0, 512), QK_SCALE * LOG2E)
    store_rope(ka_ref, project(512, 512), 1.0)
    va_ref[...] = project(1024, 512).astype(va_ref.dtype)
    store_rope(qb_ref, project(1536, 512), QK_SCALE)
    zk = _rope_block(project(2048, LANES), cos, sin_signed, first_half)
    store_padded(kb_ref, zk)
    store_padded(vb_ref, project(2176, LANES))


def _const_spec(shape):
    zeros = (0,) * len(shape)
    return pl.BlockSpec(shape, lambda *_: zeros, pipeline_mode=pl.Buffered(1))


def _ffn_in(x, cos, sin_signed, n1, wg, wu, wd, n2, win, row_tile):
    rows, d = x.shape
    pos_tiles = cos.shape[0] // row_tile
    row = lambda i: (i, 0)
    out_w = 512
    return pl.pallas_call(
        _ffn_in_kernel,
        grid=(rows // row_tile,),
        in_specs=[
            pl.BlockSpec((row_tile, d), row),
            pl.BlockSpec((row_tile, LANES), lambda i: (i % pos_tiles, 0)),
            pl.BlockSpec((row_tile, LANES), lambda i: (i % pos_tiles, 0)),
            _const_spec(n1.shape), _const_spec(wg.shape), _const_spec(wu.shape), _const_spec(wd.shape),
            _const_spec(n2.shape), _const_spec(win.shape),
        ],
        out_specs=[pl.BlockSpec((row_tile, d), row)] + [pl.BlockSpec((row_tile, out_w), row)] * 6,
        out_shape=[jax.ShapeDtypeStruct((rows, d), F32)] + [jax.ShapeDtypeStruct((rows, out_w), BF16)] * 6,
        compiler_params=pltpu.CompilerParams(dimension_semantics=("parallel",),
                                             vmem_limit_bytes=VMEM_LIMIT_ROWWISE),
        name="ffn_in",
    )(x, cos, sin_signed, n1, wg, wu, wd, n2, win)


def _diff_attn_kernel(lq1_ref, lk1_ref, lq2_ref, lk2_ref, gain_ref, q_ref, km_ref, vmt_ref, k_ref, vt_ref,
                      o_ref, qz_sc, sa_sc, sb_sc, m_sc, l_sc, acc_sc):
    lam = (jnp.exp(jnp.sum(lq1_ref[...] * lk1_ref[...], keepdims=True))
           - jnp.exp(jnp.sum(lq2_ref[...] * lk2_ref[...], keepdims=True)) + LAMBDA_INIT)
    q = q_ref[...]
    lane = lax.broadcasted_iota(jnp.int32, q.shape, 1)
    zero = jnp.zeros_like(q)
    qz_sc[0] = jnp.where(lane < HEAD_DIM, q, zero)
    qz_sc[1] = jnp.where(lane < HEAD_DIM, zero, q)

    for c in range(2):
        s = lax.dot_general(km_ref[...], qz_sc[c], _NT, preferred_element_type=F32)
        m = jnp.max(s, axis=0, keepdims=True)
        p = jnp.exp2(s - m)
        m_sc[c] = m
        l_sc[c] = jnp.sum(p, axis=0, keepdims=True)
        acc_sc[c] = jnp.dot(vmt_ref[0], p.astype(BF16), preferred_element_type=F32)

    nk, tk = vt_ref.shape[2], vt_ref.shape[4]

    def produce(t, s_sc):
        kt = k_ref[pl.ds(pl.multiple_of(t * tk, tk), tk), :]
        maxima = []
        for c in range(2):
            s = lax.dot_general(kt, qz_sc[c], _NT, preferred_element_type=F32)
            s_sc[c] = s
            maxima.append(jnp.max(s, axis=0, keepdims=True))
        return tuple(maxima)

    def consume(t, s_sc, maxima):
        vt = vt_ref[0, 0, t]
        for c in range(2):
            m_prev = m_sc[c]
            m_new = jnp.maximum(m_prev, maxima[c])
            alpha = jnp.exp2(m_prev - m_new)
            p = jnp.exp2(s_sc[c] - m_new)
            l_sc[c] = alpha * l_sc[c] + jnp.sum(p, axis=0, keepdims=True)
            acc_sc[c] = alpha * acc_sc[c] + jnp.dot(vt, p.astype(BF16), preferred_element_type=F32)
            m_sc[c] = m_new

    def pair(j, maxima_a):
        t = 2 * j
        maxima_b = produce(t + 1, sb_sc)
        consume(t, sa_sc, maxima_a)
        maxima_a = produce(t + 2, sa_sc)
        consume(t + 1, sb_sc, maxima_b)
        return maxima_a

    maxima_a = lax.fori_loop(0, nk // 2 - 1, pair, produce(0, sa_sc))
    maxima_b = produce(nk - 1, sb_sc)
    consume(nk - 2, sa_sc, maxima_a)
    consume(nk - 1, sb_sc, maxima_b)

    a = acc_sc[0] / l_sc[0] - lam * (acc_sc[1] / l_sc[1])
    ms = jnp.mean(a * a, axis=0, keepdims=True)
    y = a * lax.rsqrt(ms + EPS) * gain_ref[...] * (1.0 - LAMBDA_INIT)
    o_ref[...] = y.T.astype(o_ref.dtype)


def _diff_attn(lams, gain_col, qa, ka_meta, vmt, ka, vt_tiles, batch):
    rows = qa.shape[0]
    seq = rows // batch
    nq = seq // DIFF_TQ
    nk, vdim, tk = vt_tiles.shape[2:]
    lam_spec = pl.BlockSpec((1, HEAD_DIM), lambda b, h, i: (0, 0))
    return pl.pallas_call(
        _diff_attn_kernel,
        grid=(batch, DIFF_HEADS, nq),
        in_specs=[
            lam_spec, lam_spec, lam_spec, lam_spec,
            pl.BlockSpec((vdim, 1), lambda b, h, i: (0, 0)),
            pl.BlockSpec((DIFF_TQ, LANES), lambda b, h, i: (b * nq + i, h)),
            pl.BlockSpec((N_META, LANES), lambda b, h, i: (0, h)),
            pl.BlockSpec((1, vdim, N_META), lambda b, h, i: (h, 0, 0)),
            pl.BlockSpec((seq, LANES), lambda b, h, i: (b, h)),
            pl.BlockSpec((1, 1, nk, vdim, tk), lambda b, h, i: (b, h, 0, 0, 0)),
        ],
        out_specs=pl.BlockSpec((DIFF_TQ, LANES), lambda b, h, i: (b * nq + i, h)),
        out_shape=jax.ShapeDtypeStruct((rows, DIFF_HEADS * vdim), BF16),
        scratch_shapes=[pltpu.VMEM((2, DIFF_TQ, LANES), BF16),
                        pltpu.VMEM((2, tk, DIFF_TQ), F32), pltpu.VMEM((2, tk, DIFF_TQ), F32),
                        pltpu.VMEM((2, 1, DIFF_TQ), F32), pltpu.VMEM((2, 1, DIFF_TQ), F32),
                        pltpu.VMEM((2, vdim, DIFF_TQ), F32)],
        compiler_params=pltpu.CompilerParams(dimension_semantics=("parallel", "parallel", "arbitrary"),
                                             vmem_limit_bytes=VMEM_LIMIT_ATTN),
        name="diff_attn",
    )(*lams, gain_col, qa, ka_meta, vmt, ka, vt_tiles)


def _win_attn_kernel(sink_ref, q_ref, km_ref, vm_ref, k_ref, v_ref, o_ref):
    g = pl.program_id(1)
    i = pl.program_id(2)
    seq = k_ref.shape[0]
    t0 = i * WIN_TQ
    ks = pl.multiple_of(jnp.clip(t0 - WINDOW, 0, seq - WIN_SPAN), WINDOW)
    kt = k_ref[pl.ds(ks, WIN_SPAN), :]
    vt = v_ref[pl.ds(ks, WIN_SPAN), :]
    qpos = t0 + lax.broadcasted_iota(jnp.int32, (WIN_TQ, WIN_SPAN), 0)
    kpos = ks + lax.broadcasted_iota(jnp.int32, (WIN_TQ, WIN_SPAN), 1)
    valid = jnp.abs(kpos - qpos) <= WINDOW
    for pair in range(WIN_GROUP // 2):
        qp = q_ref[:, pair * LANES:(pair + 1) * LANES]
        o_pair = None
        for par in range(2):
            cols = slice(par * LANES, (par + 1) * LANES)
            sink = sink_ref[g * WIN_GROUP + 2 * pair + par]
            s = lax.dot_general(qp, kt[:, cols], _NT, preferred_element_type=F32)
            s = jnp.where(valid, s, NEG)
            sm = lax.dot_general(qp, km_ref[:, cols], _NT, preferred_element_type=F32)
            m = jnp.maximum(jnp.maximum(jnp.max(s, axis=1, keepdims=True),
                                        jnp.max(sm, axis=1, keepdims=True)), sink)
            p = jnp.exp(s - m)
            pm = jnp.exp(sm - m)
            l = jnp.sum(p, axis=1, keepdims=True) + jnp.sum(pm, axis=1, keepdims=True) + jnp.exp(sink - m)
            o = (jnp.dot(p.astype(BF16), vt[:, cols], preferred_element_type=F32)
                 + jnp.dot(pm.astype(BF16), vm_ref[:, cols], preferred_element_type=F32)) / l
            o_pair = o if o_pair is None else o_pair + o
        o_ref[:, pair * LANES:(pair + 1) * LANES] = o_pair.astype(o_ref.dtype)


def _win_attn(sink, qb, kb_meta, vb_meta, kb, vb, batch):
    rows = qb.shape[0]
    seq = rows // batch
    nq = seq // WIN_TQ
    width = 2 * LANES
    return pl.pallas_call(
        _win_attn_kernel,
        grid=(batch, WIN_KV_HEADS, nq),
        in_specs=[
            pl.BlockSpec(memory_space=pltpu.SMEM),
            pl.BlockSpec((WIN_TQ, width), lambda b, g, i: (b * nq + i, g)),
            pl.BlockSpec((N_META, width), lambda b, g, i: (0, g)),
            pl.BlockSpec((N_META, width), lambda b, g, i: (0, g)),
            pl.BlockSpec((seq, width), lambda b, g, i: (b, g)),
            pl.BlockSpec((seq, width), lambda b, g, i: (b, g)),
        ],
        out_specs=pl.BlockSpec((WIN_TQ, width), lambda b, g, i: (b * nq + i, g)),
        out_shape=jax.ShapeDtypeStruct((rows, WIN_KV_HEADS * width), BF16),
        compiler_params=pltpu.CompilerParams(dimension_semantics=("parallel", "parallel", "arbitrary"),
                                             vmem_limit_bytes=VMEM_LIMIT_ATTN),
        name="win_attn",
    )(sink, qb, kb_meta, vb_meta, kb, vb)


def _out_ffn_kernel(h_ref, oa_ref, ob_ref, wn_ref, wo_ref, n_ref, wg_ref, wu_ref, wd_ref, fn_ref, out_ref):
    half = oa_ref.shape[1]
    ob = _rms(ob_ref[...].astype(F32), wn_ref[...]).astype(BF16)
    mix = (jnp.dot(oa_ref[...], wo_ref[:half, :], preferred_element_type=F32)
           + jnp.dot(ob, wo_ref[half:, :], preferred_element_type=F32))
    h = h_ref[...] + mix
    xn = _rms(h, n_ref[...]).astype(BF16)
    h = h + 0.5 * _swiglu(xn, wg_ref, wu_ref, wd_ref)
    out_ref[...] = _rms(h, fn_ref[...])


def _out_ffn(h1, oa, ob, wn, wo, n, wg, wu, wd, fn):
    rows, d = h1.shape
    half = oa.shape[1]
    row = lambda i: (i, 0)
    return pl.pallas_call(
        _out_ffn_kernel,
        grid=(rows // ROW_TILE,),
        in_specs=[
            pl.BlockSpec((ROW_TILE, d), row), pl.BlockSpec((ROW_TILE, half), row), pl.BlockSpec((ROW_TILE, half), row),
            _const_spec(wn.shape), _const_spec(wo.shape), _const_spec(n.shape),
            _const_spec(wg.shape), _const_spec(wu.shape), _const_spec(wd.shape), _const_spec(fn.shape),
        ],
        out_specs=pl.BlockSpec((ROW_TILE, d), row),
        out_shape=jax.ShapeDtypeStruct((rows, d), F32),
        compiler_params=pltpu.CompilerParams(dimension_semantics=("parallel",),
                                             vmem_limit_bytes=VMEM_LIMIT_ROWWISE),
        name="out_ffn",
    )(h1, oa, ob, wn, wo, n, wg, wu, wd, fn)


def _rope_tables(length):
    pos = jnp.arange(length, dtype=F32)
    inv = ROPE_THETA ** (-jnp.arange(0, HEAD_DIM, 2, dtype=F32) / HEAD_DIM)
    ang = pos[:, None] * inv[None, :]
    cos, sin = jnp.cos(ang), jnp.sin(ang)
    return jnp.tile(cos, (1, 4)), jnp.tile(jnp.concatenate([-sin, sin], axis=1), (1, 2))


def kernel(x, meta_tokens, ffn1_norm, ffn1_w_gate, ffn1_w_up, ffn1_w_down, mix_norm, w_in, lambda_q1, lambda_k1, lambda_q2, lambda_k2, diff_norm, win_sink, win_norm, w_out, ffn2_norm, ffn2_w_gate, ffn2_w_up, ffn2_w_down, final_norm):
    batch, seq, d = x.shape
    assert ffn1_norm.shape[0] == 1, "single layer only"
    assert seq % ROW_TILE == 0 and seq % DIFF_TQ == 0 and seq % DIFF_TK == 0 and seq % WIN_TQ == 0

    cos, sin_signed = _rope_tables(N_META + seq)
    ffn_in_weights = (ffn1_norm, ffn1_w_gate[0].astype(BF16), ffn1_w_up[0].astype(BF16),
                      ffn1_w_down[0].astype(BF16), mix_norm, w_in[0].astype(BF16))

    meta = _ffn_in(meta_tokens.astype(x.dtype), cos[:N_META], sin_signed[:N_META], *ffn_in_weights, N_META)
    real = _ffn_in(x.reshape(batch * seq, d), cos[N_META:], sin_signed[N_META:], *ffn_in_weights, ROW_TILE)
    _, _, ka_m, va_m, _, kb_m, vb_m = meta
    h1, qa, ka, va, qb, kb, vb = real

    vdim = 2 * HEAD_DIM
    nk = seq // DIFF_TK
    vt_tiles = va.reshape(batch, nk, DIFF_TK, DIFF_HEADS, vdim).transpose(0, 3, 1, 4, 2)
    vmt = va_m.reshape(N_META, DIFF_HEADS, vdim).transpose(1, 2, 0)
    lams = (lambda_q1, lambda_k1, lambda_q2, lambda_k2)
    out_a = _diff_attn(lams, diff_norm.reshape(vdim, 1), qa, ka_m, vmt, ka, vt_tiles, batch)

    out_b = _win_attn(win_sink.reshape(-1), qb, kb_m, vb_m, kb, vb, batch)

    out = _out_ffn(h1, out_a, out_b, win_norm, w_out[0].astype(BF16), ffn2_norm,
                   ffn2_w_gate[0].astype(BF16), ffn2_w_up[0].astype(BF16), ffn2_w_down[0].astype(BF16),
                   final_norm.reshape(1, d))
    return out.reshape(batch, seq, d)
```

```python
import functools
import math

import jax
import jax.numpy as jnp
from jax import lax
from jax.experimental import pallas as pl
from jax.experimental.pallas import tpu as pltpu

N_META = 16
HEAD_DIM = 64
DIFF_HEADS = 4
WIN_KV_HEADS = 2
WIN_GROUP = 4
WINDOW = 128
ROPE_THETA = 10000.0
EPS = 1e-6
NEG = -1e30
LAMBDA_INIT = 0.8 - 0.6 * math.exp(-0.3 * 0)
QK_SCALE = HEAD_DIM ** -0.5
LOG2E = math.log2(math.e)

LANES = 128
BF16_SUBLANES = 16
VMEM_LIMIT_ROWWISE = 56 * 1024 * 1024
VMEM_LIMIT_ATTN = 40 * 1024 * 1024

ROW_TILE = 512
FF_CHUNK = 1408
DIFF_TQ = 512
DIFF_TK = 1024
WIN_TQ = 256
WIN_SPAN = WIN_TQ + 2 * WINDOW
WIN_QBLOCK = 4096

F32 = jnp.float32
BF16 = jnp.bfloat16
_NT = (((1,), (1,)), ((), ()))


def _rms(x, gain):
    return x * lax.rsqrt(jnp.mean(x * x, axis=-1, keepdims=True) + EPS) * gain


def _swiglu(xn, wg_ref, wu_ref, wd_ref):
    d_ff = wg_ref.shape[1]
    acc = None
    for c in range(d_ff // FF_CHUNK):
        sl = slice(c * FF_CHUNK, (c + 1) * FF_CHUNK)
        g = jnp.dot(xn, wg_ref[:, sl], preferred_element_type=F32)
        u = jnp.dot(xn, wu_ref[:, sl], preferred_element_type=F32)
        a = (g * jax.nn.sigmoid(g) * u).astype(BF16)
        d = jnp.dot(a, wd_ref[sl, :], preferred_element_type=F32)
        acc = d if acc is None else acc + d
    return acc


def _rope_block(x, cos, sin_signed, first_half):
    partner = jnp.where(first_half, pltpu.roll(x, LANES - 32, 1), pltpu.roll(x, 32, 1))
    return x * cos + partner * sin_signed


def _ffn_in_kernel(x_ref, cos_ref, sin_ref, n1_ref, wg_ref, wu_ref, wd_ref, n2_ref, win_ref,
                   h_ref, qa_ref, ka_ref, va_ref, qb_ref, kb_ref, vb_ref, *, v_transposed):
    x = x_ref[...]
    xn = _rms(x, n1_ref[...]).astype(BF16)
    h = x + 0.5 * _swiglu(xn, wg_ref, wu_ref, wd_ref)
    h_ref[...] = h
    u = _rms(h, n2_ref[...]).astype(BF16)

    rows = x.shape[0]
    cos = cos_ref[...]
    sin_signed = sin_ref[...]
    lane = lax.broadcasted_iota(jnp.int32, (rows, LANES), 1)
    first_half = (lane % HEAD_DIM) < (HEAD_DIM // 2)
    low = lane < HEAD_DIM

    def project(col0, width):
        return jnp.dot(u, win_ref[:, col0:col0 + width], preferred_element_type=F32)

    def store_rope(dst_ref, z, scale):
        for i in range(z.shape[1] // LANES):
            blk = _rope_block(z[:, i * LANES:(i + 1) * LANES], cos, sin_signed, first_half)
            if scale != 1.0:
                blk = blk * scale
            dst_ref[:, i * LANES:(i + 1) * LANES] = blk.astype(dst_ref.dtype)

    def store_padded(dst_ref, z):
        swapped = pltpu.roll(z, HEAD_DIM, 1)
        zero = jnp.zeros_like(z)
        parts = (jnp.where(low, z, zero), jnp.where(low, zero, swapped),
                 jnp.where(low, swapped, zero), jnp.where(low, zero, z))
        for i, part in enumerate(parts):
            dst_ref[:, i * LANES:(i + 1) * LANES] = part.astype(dst_ref.dtype)

    def store_padded_transposed(dst_ref, z):
        zt = z.T
        swapped = jnp.concatenate([zt[HEAD_DIM:], zt[:HEAD_DIM]], axis=0)
        top = lax.broadcasted_iota(jnp.int32, zt.shape, 0) < HEAD_DIM
        zero = jnp.zeros_like(zt)
        parts = (jnp.where(top, zt, zero), jnp.where(top, zero, swapped),
                 jnp.where(top, swapped, zero), jnp.where(top, zero, zt))
        for i, part in enumerate(parts):
            dst_ref[i * LANES:(i + 1) * LANES, :] = part.astype(dst_ref.dtype)

    store_rope(qa_ref, project(0, 512), QK_SCALE * LOG2E)
    store_rope(ka_ref, project(512, 512), 1.0)
    store_rope(qb_ref, project(1536, 512), QK_SCALE * LOG2E)
    zk = _rope_block(project(2048, LANES), cos, sin_signed, first_half)
    store_padded(kb_ref, zk)
    zva = project(1024, 512)
    zvb = project(2176, LANES)
    if v_transposed:
        for i in range(zva.shape[1] // LANES):
            va_ref[i * LANES:(i + 1) * LANES, :] = zva[:, i * LANES:(i + 1) * LANES].T.astype(va_ref.dtype)
        store_padded_transposed(vb_ref, zvb)
    else:
        va_ref[...] = zva.astype(va_ref.dtype)
        store_padded(vb_ref, zvb)


def _const_spec(shape):
    zeros = (0,) * len(shape)
    return pl.BlockSpec(shape, lambda *_: zeros, pipeline_mode=pl.Buffered(1))


def _ffn_in(x, cos, sin_signed, n1, wg, wu, wd, n2, win, row_tile, v_transposed):
    rows, d = x.shape
    pos_tiles = cos.shape[0] // row_tile
    row = lambda i: (i, 0)
    out_w = 512
    wide = pl.BlockSpec((row_tile, out_w), row)
    wide_shape = jax.ShapeDtypeStruct((rows, out_w), BF16)
    v_spec = pl.BlockSpec((out_w, row_tile), lambda i: (0, i)) if v_transposed else wide
    v_shape = jax.ShapeDtypeStruct((out_w, rows), BF16) if v_transposed else wide_shape
    return pl.pallas_call(
        functools.partial(_ffn_in_kernel, v_transposed=v_transposed),
        grid=(rows // row_tile,),
        in_specs=[
            pl.BlockSpec((row_tile, d), row),
            pl.BlockSpec((row_tile, LANES), lambda i: (i % pos_tiles, 0)),
            pl.BlockSpec((row_tile, LANES), lambda i: (i % pos_tiles, 0)),
            _const_spec(n1.shape), _const_spec(wg.shape), _const_spec(wu.shape), _const_spec(wd.shape),
            _const_spec(n2.shape), _const_spec(win.shape),
        ],
        out_specs=[pl.BlockSpec((row_tile, d), row), wide, wide, v_spec, wide, wide, v_spec],
        out_shape=[jax.ShapeDtypeStruct((rows, d), F32), wide_shape, wide_shape, v_shape, wide_shape, wide_shape,
                   v_shape],
        compiler_params=pltpu.CompilerParams(dimension_semantics=("parallel",),
                                             vmem_limit_bytes=VMEM_LIMIT_ROWWISE),
        name="ffn_in",
    )(x, cos, sin_signed, n1, wg, wu, wd, n2, win)


def _diff_attn_kernel(lq1_ref, lk1_ref, lq2_ref, lk2_ref, gain_ref, q_ref, km_ref, vmt_ref, k_ref, vt_ref,
                      o_ref, qz_sc, sa_sc, sb_sc, m_sc, acc_sc):
    lam = (jnp.exp(jnp.sum(lq1_ref[...] * lk1_ref[...], keepdims=True))
           - jnp.exp(jnp.sum(lq2_ref[...] * lk2_ref[...], keepdims=True)) + LAMBDA_INIT)
    tq = sa_sc.shape[2]
    tk = sa_sc.shape[1]
    vdim = vt_ref.shape[0]
    nq = q_ref.shape[0] // tq
    nk = vt_ref.shape[1] // tk

    def with_ones(vt):
        return jnp.concatenate([vt, jnp.ones((BF16_SUBLANES, vt.shape[1]), BF16)], axis=0)

    def load_q(i, slot):
        q = q_ref[pl.ds(pl.multiple_of(i * tq, tq), tq), :]
        lane = lax.broadcasted_iota(jnp.int32, q.shape, 1)
        zero = jnp.zeros_like(q)
        qz_sc[slot, 0] = jnp.where(lane < HEAD_DIM, q, zero)
        qz_sc[slot, 1] = jnp.where(lane < HEAD_DIM, zero, q)

    def produce(t, s_sc, slot):
        kt = k_ref[pl.ds(pl.multiple_of(t * tk, tk), tk), :]
        maxima = []
        for c in range(2):
            s = lax.dot_general(kt, qz_sc[slot, c], _NT, preferred_element_type=F32)
            s_sc[c] = s
            maxima.append(jnp.max(s, axis=0, keepdims=True))
        return tuple(maxima)

    def fold(c, s, s_max, vt):
        m_prev = m_sc[c]
        m_new = jnp.maximum(m_prev, s_max)
        alpha = jnp.exp2(m_prev - m_new)
        p = jnp.exp2(s - m_new)
        acc_sc[c] = alpha * acc_sc[c] + jnp.dot(vt, p.astype(BF16), preferred_element_type=F32)
        m_sc[c] = m_new

    def consume(t, s_sc, maxima):
        vt = with_ones(vt_ref[:, pl.ds(pl.multiple_of(t * tk, tk), tk)])
        for c in range(2):
            fold(c, s_sc[c], maxima[c], vt)

    def consume_meta(slot):
        vmt = with_ones(vmt_ref[...])
        for c in range(2):
            s = lax.dot_general(km_ref[...], qz_sc[slot, c], _NT, preferred_element_type=F32)
            fold(c, s, jnp.max(s, axis=0, keepdims=True), vmt)

    def finalize(i):
        def normalised(c):
            acc = acc_sc[c]
            return acc[:vdim] * (1.0 / acc[vdim:vdim + 1])

        a = normalised(0) - lam * normalised(1)
        ms = jnp.mean(a * a, axis=0, keepdims=True)
        y = a * lax.rsqrt(ms + EPS) * gain_ref[...] * (1.0 - LAMBDA_INIT)
        o_ref[pl.ds(pl.multiple_of(i * tq, tq), tq), :] = y.T.astype(o_ref.dtype)

    def query_tile(i, maxima_a):
        slot = i % 2
        for c in range(2):
            m_sc[c] = jnp.full(m_sc.shape[1:], NEG, F32)
            acc_sc[c] = jnp.zeros(acc_sc.shape[1:], F32)

        def pair(j, maxima_a):
            t = 2 * j
            maxima_b = produce(t + 1, sb_sc, slot)
            consume(t, sa_sc, maxima_a)
            maxima_a = produce(t + 2, sa_sc, slot)
            consume(t + 1, sb_sc, maxima_b)
            return maxima_a

        maxima_a = lax.fori_loop(0, nk // 2 - 1, pair, maxima_a)
        consume_meta(slot)
        maxima_b = produce(nk - 1, sb_sc, slot)
        consume(nk - 2, sa_sc, maxima_a)
        load_q(jnp.minimum(i + 1, nq - 1), 1 - slot)
        maxima_next = produce(0, sa_sc, 1 - slot)
        consume(nk - 1, sb_sc, maxima_b)
        finalize(i)
        return maxima_next

    load_q(0, 0)
    lax.fori_loop(0, nq, query_tile, produce(0, sa_sc, 0))


def _diff_attn(lams, gain_col, qa, ka_meta, vat_meta, ka, vat, batch):
    rows = qa.shape[0]
    seq = rows // batch
    vdim = 2 * HEAD_DIM
    tq, tk = DIFF_TQ, DIFF_TK
    assert (seq // tk) % 2 == 0
    lam_spec = pl.BlockSpec((1, HEAD_DIM), lambda b, h: (0, 0))
    return pl.pallas_call(
        _diff_attn_kernel,
        grid=(batch, DIFF_HEADS),
        in_specs=[
            lam_spec, lam_spec, lam_spec, lam_spec,
            pl.BlockSpec((vdim, 1), lambda b, h: (0, 0)),
            pl.BlockSpec((seq, LANES), lambda b, h: (b, h)),
            pl.BlockSpec((N_META, LANES), lambda b, h: (0, h)),
            pl.BlockSpec((vdim, N_META), lambda b, h: (h, 0)),
            pl.BlockSpec((seq, LANES), lambda b, h: (b, h)),
            pl.BlockSpec((vdim, seq), lambda b, h: (h, b)),
        ],
        out_specs=pl.BlockSpec((seq, LANES), lambda b, h: (b, h)),
        out_shape=jax.ShapeDtypeStruct((rows, DIFF_HEADS * vdim), BF16),
        scratch_shapes=[pltpu.VMEM((2, 2, tq, LANES), BF16),
                        pltpu.VMEM((2, tk, tq), F32), pltpu.VMEM((2, tk, tq), F32),
                        pltpu.VMEM((2, 1, tq), F32),
                        pltpu.VMEM((2, vdim + BF16_SUBLANES, tq), F32)],
        compiler_params=pltpu.CompilerParams(dimension_semantics=("parallel", "parallel"),
                                             vmem_limit_bytes=VMEM_LIMIT_ATTN),
        name="diff_attn",
    )(*lams, gain_col, qa, ka_meta, vat_meta, ka, vat)


def _win_attn_kernel(sink_ref, q_ref, km_ref, vmt_ref, k_ref, vt_ref, o_ref,
                     delta_sc, sa_sc, sb_sc, sma_sc, smb_sc):
    g = pl.program_id(1)
    blk = pl.program_id(2)
    seq = k_ref.shape[0]
    tq, span = WIN_TQ, WIN_SPAN
    n = 2 * tq
    nt = q_ref.shape[0] // tq
    delta_sc[...] = (lax.broadcasted_iota(jnp.int32, (span, tq), 0)
                     - lax.broadcasted_iota(jnp.int32, (span, tq), 1))
    first_pair = lax.broadcasted_iota(jnp.int32, (1, n), 1) < tq
    sinks = [jnp.where(first_pair, sink_ref[g * WIN_GROUP + par], sink_ref[g * WIN_GROUP + 2 + par]) * LOG2E
             for par in range(2)]

    def key_start(i):
        t0 = blk * (nt * tq) + i * tq
        return t0, pl.multiple_of(jnp.clip(t0 - WINDOW, 0, seq - span), WINDOW)

    def produce(i, s_sc, sm_sc):
        t0, ks = key_start(i)
        off = ks - t0
        delta = delta_sc[...]
        valid = (delta >= -WINDOW - off) & (delta <= WINDOW - off)
        rows = pl.ds(pl.multiple_of(i * tq, tq), tq)
        q_cat = jnp.concatenate([q_ref[rows, :LANES], q_ref[rows, LANES:]], axis=0)
        maxima = []
        for par in range(2):
            part = slice(par * LANES, (par + 1) * LANES)
            s = lax.dot_general(k_ref[pl.ds(ks, span), part], q_cat, _NT, preferred_element_type=F32)
            s = jnp.concatenate([jnp.where(valid, s[:, :tq], NEG), jnp.where(valid, s[:, tq:], NEG)], axis=1)
            sm = lax.dot_general(km_ref[:, part], q_cat, _NT, preferred_element_type=F32)
            s_sc[par] = s
            sm_sc[par] = sm
            maxima.append(jnp.maximum(jnp.maximum(jnp.max(s, axis=0, keepdims=True),
                                                  jnp.max(sm, axis=0, keepdims=True)), sinks[par]))
        return tuple(maxima)

    def consume(i, s_sc, sm_sc, maxima):
        _, ks = key_start(i)
        o_t = None
        for par in range(2):
            part = slice(par * LANES, (par + 1) * LANES)
            m = maxima[par]
            p = jnp.exp2(s_sc[par] - m)
            pm = jnp.exp2(sm_sc[par] - m)
            l = jnp.sum(p, axis=0, keepdims=True) + jnp.sum(pm, axis=0, keepdims=True) + jnp.exp2(sinks[par] - m)
            o = (jnp.dot(vt_ref[part, pl.ds(ks, span)], p.astype(BF16), preferred_element_type=F32)
                 + jnp.dot(vmt_ref[part, :], pm.astype(BF16), preferred_element_type=F32)) * (1.0 / l)
            o_t = o if o_t is None else o_t + o
        rows = pl.ds(pl.multiple_of(i * tq, tq), tq)
        for pair in range(WIN_GROUP // 2):
            o_ref[rows, pair * LANES:(pair + 1) * LANES] = o_t[:, pair * tq:(pair + 1) * tq].T.astype(o_ref.dtype)

    def pair_of_tiles(j, maxima_a):
        i = 2 * j
        maxima_b = produce(i + 1, sb_sc, smb_sc)
        consume(i, sa_sc, sma_sc, maxima_a)
        maxima_a = produce(i + 2, sa_sc, sma_sc)
        consume(i + 1, sb_sc, smb_sc, maxima_b)
        return maxima_a

    maxima_a = lax.fori_loop(0, nt // 2 - 1, pair_of_tiles, produce(0, sa_sc, sma_sc))
    maxima_b = produce(nt - 1, sb_sc, smb_sc)
    consume(nt - 2, sa_sc, sma_sc, maxima_a)
    consume(nt - 1, sb_sc, smb_sc, maxima_b)


def _win_attn(sink, qb, kb_meta, vbt_meta, kb, vbt, batch):
    rows = qb.shape[0]
    seq = rows // batch
    nblk = seq // WIN_QBLOCK
    assert (WIN_QBLOCK // WIN_TQ) % 2 == 0
    width = 2 * LANES
    n = 2 * WIN_TQ
    return pl.pallas_call(
        _win_attn_kernel,
        grid=(batch, WIN_KV_HEADS, nblk),
        in_specs=[
            pl.BlockSpec(memory_space=pltpu.SMEM),
            pl.BlockSpec((WIN_QBLOCK, width), lambda b, g, i: (b * nblk + i, g)),
            pl.BlockSpec((N_META, width), lambda b, g, i: (0, g)),
            pl.BlockSpec((width, N_META), lambda b, g, i: (g, 0)),
            pl.BlockSpec((seq, width), lambda b, g, i: (b, g)),
            pl.BlockSpec((width, seq), lambda b, g, i: (g, b)),
        ],
        out_specs=pl.BlockSpec((WIN_QBLOCK, width), lambda b, g, i: (b * nblk + i, g)),
        out_shape=jax.ShapeDtypeStruct((rows, WIN_KV_HEADS * width), BF16),
        scratch_shapes=[pltpu.VMEM((WIN_SPAN, WIN_TQ), jnp.int32),
                        pltpu.VMEM((2, WIN_SPAN, n), F32), pltpu.VMEM((2, WIN_SPAN, n), F32),
                        pltpu.VMEM((2, N_META, n), F32), pltpu.VMEM((2, N_META, n), F32)],
        compiler_params=pltpu.CompilerParams(dimension_semantics=("parallel", "parallel", "arbitrary"),
                                             vmem_limit_bytes=VMEM_LIMIT_ATTN),
        name="win_attn",
    )(sink, qb, kb_meta, vbt_meta, kb, vbt)


def _out_ffn_kernel(h_ref, oa_ref, ob_ref, wn_ref, wo_ref, n_ref, wg_ref, wu_ref, wd_ref, fn_ref, out_ref):
    half = oa_ref.shape[1]
    ob = _rms(ob_ref[...].astype(F32), wn_ref[...]).astype(BF16)
    mix = (jnp.dot(oa_ref[...], wo_ref[:half, :], preferred_element_type=F32)
           + jnp.dot(ob, wo_ref[half:, :], preferred_element_type=F32))
    h = h_ref[...] + mix
    xn = _rms(h, n_ref[...]).astype(BF16)
    h = h + 0.5 * _swiglu(xn, wg_ref, wu_ref, wd_ref)
    out_ref[...] = _rms(h, fn_ref[...])


def _out_ffn(h1, oa, ob, wn, wo, n, wg, wu, wd, fn):
    rows, d = h1.shape
    half = oa.shape[1]
    row = lambda i: (i, 0)
    return pl.pallas_call(
        _out_ffn_kernel,
        grid=(rows // ROW_TILE,),
        in_specs=[
            pl.BlockSpec((ROW_TILE, d), row), pl.BlockSpec((ROW_TILE, half), row), pl.BlockSpec((ROW_TILE, half), row),
            _const_spec(wn.shape), _const_spec(wo.shape), _const_spec(n.shape),
            _const_spec(wg.shape), _const_spec(wu.shape), _const_spec(wd.shape), _const_spec(fn.shape),
        ],
        out_specs=pl.BlockSpec((ROW_TILE, d), row),
        out_shape=jax.ShapeDtypeStruct((rows, d), F32),
        compiler_params=pltpu.CompilerParams(dimension_semantics=("parallel",),
                                             vmem_limit_bytes=VMEM_LIMIT_ROWWISE),
        name="out_ffn",
    )(h1, oa, ob, wn, wo, n, wg, wu, wd, fn)


def _rope_tables(length):
    pos = jnp.arange(length, dtype=F32)
    inv = ROPE_THETA ** (-jnp.arange(0, HEAD_DIM, 2, dtype=F32) / HEAD_DIM)
    ang = pos[:, None] * inv[None, :]
    cos, sin = jnp.cos(ang), jnp.sin(ang)
    return jnp.tile(cos, (1, 4)), jnp.tile(jnp.concatenate([-sin, sin], axis=1), (1, 2))


def kernel(x, meta_tokens, ffn1_norm, ffn1_w_gate, ffn1_w_up, ffn1_w_down, mix_norm, w_in, lambda_q1, lambda_k1, lambda_q2, lambda_k2, diff_norm, win_sink, win_norm, w_out, ffn2_norm, ffn2_w_gate, ffn2_w_up, ffn2_w_down, final_norm):
    batch, seq, d = x.shape
    assert ffn1_norm.shape[0] == 1, "single layer only"
    assert seq % ROW_TILE == 0 and seq % DIFF_TQ == 0 and seq % DIFF_TK == 0 and seq % WIN_QBLOCK == 0

    cos, sin_signed = _rope_tables(N_META + seq)
    ffn_in_weights = (ffn1_norm, ffn1_w_gate[0].astype(BF16), ffn1_w_up[0].astype(BF16),
                      ffn1_w_down[0].astype(BF16), mix_norm, w_in[0].astype(BF16))

    meta = _ffn_in(meta_tokens.astype(x.dtype), cos[:N_META], sin_signed[:N_META], *ffn_in_weights, N_META,
                   v_transposed=False)
    real = _ffn_in(x.reshape(batch * seq, d), cos[N_META:], sin_signed[N_META:], *ffn_in_weights, ROW_TILE,
                   v_transposed=True)
    _, _, ka_m, va_m, _, kb_m, vb_m = meta
    h1, qa, ka, vat, qb, kb, vbt = real

    lams = (lambda_q1, lambda_k1, lambda_q2, lambda_k2)
    out_a = _diff_attn(lams, diff_norm.reshape(2 * HEAD_DIM, 1), qa, ka_m, va_m.T, ka, vat, batch)

    out_b = _win_attn(win_sink.reshape(-1), qb, kb_m, vb_m.T, kb, vbt, batch)

    out = _out_ffn(h1, out_a, out_b, win_norm, w_out[0].astype(BF16), ffn2_norm,
                   ffn2_w_gate[0].astype(BF16), ffn2_w_up[0].astype(BF16), ffn2_w_down[0].astype(BF16),
                   final_norm.reshape(1, d))
    return out.reshape(batch, seq, d)
```

```python
import functools
import math

import jax
import jax.numpy as jnp
from jax import lax
from jax.experimental import pallas as pl
from jax.experimental.pallas import tpu as pltpu

N_META = 16
HEAD_DIM = 64
DIFF_HEADS = 4
WIN_KV_HEADS = 2
WIN_GROUP = 4
WINDOW = 128
ROPE_THETA = 10000.0
EPS = 1e-6
NEG = -1e30
LAMBDA_INIT = 0.8 - 0.6 * math.exp(-0.3 * 0)
QK_SCALE = HEAD_DIM ** -0.5
LOG2E = math.log2(math.e)

LANES = 128
BF16_SUBLANES = 16
VMEM_LIMIT_ROWWISE = 56 * 1024 * 1024
VMEM_LIMIT_ATTN = 40 * 1024 * 1024

ROW_TILE = 512
MXU_WIDTH = 256
FF_CHUNK = 6 * MXU_WIDTH
DIFF_TQ = 512
DIFF_TK = 1024
WIN_TQ = 128
WIN_SPAN = WIN_TQ + 2 * WINDOW
WIN_QBLOCK = 4096

F32 = jnp.float32
BF16 = jnp.bfloat16
_NT = (((1,), (1,)), ((), ()))


def _rms(x, gain):
    return x * lax.rsqrt(jnp.mean(x * x, axis=-1, keepdims=True) + EPS) * gain


def _swiglu(xn, wg_ref, wu_ref, wd_ref):
    d_ff = wg_ref.shape[1]
    acc = None
    for start in range(0, d_ff, FF_CHUNK):
        sl = slice(start, min(start + FF_CHUNK, d_ff))
        g = jnp.dot(xn, wg_ref[:, sl], preferred_element_type=F32)
        u = jnp.dot(xn, wu_ref[:, sl], preferred_element_type=F32)
        a = (g * jax.nn.sigmoid(g) * u).astype(BF16)
        d = jnp.dot(a, wd_ref[sl, :], preferred_element_type=F32)
        acc = d if acc is None else acc + d
    return acc


def _rope_block(x, cos, sin_signed, first_half):
    partner = jnp.where(first_half, pltpu.roll(x, LANES - 32, 1), pltpu.roll(x, 32, 1))
    return x * cos + partner * sin_signed


def _ffn_in_kernel(x_ref, cos_ref, sin_ref, n1_ref, wg_ref, wu_ref, wd_ref, n2_ref, win_ref,
                   h_ref, qa_ref, ka_ref, va_ref, qb_ref, kb_ref, vb_ref, *, v_transposed):
    x = x_ref[...]
    xn = _rms(x, n1_ref[...]).astype(BF16)
    h = x + 0.5 * _swiglu(xn, wg_ref, wu_ref, wd_ref)
    h_ref[...] = h
    u = _rms(h, n2_ref[...]).astype(BF16)

    rows = x.shape[0]
    cos = cos_ref[...]
    sin_signed = sin_ref[...]
    lane = lax.broadcasted_iota(jnp.int32, (rows, LANES), 1)
    first_half = (lane % HEAD_DIM) < (HEAD_DIM // 2)
    low = lane < HEAD_DIM

    def project(col0, width):
        return jnp.dot(u, win_ref[:, col0:col0 + width], preferred_element_type=F32)

    def store_rope(dst_ref, z, scale):
        for i in range(z.shape[1] // LANES):
            blk = _rope_block(z[:, i * LANES:(i + 1) * LANES], cos, sin_signed, first_half)
            if scale != 1.0:
                blk = blk * scale
            dst_ref[:, i * LANES:(i + 1) * LANES] = blk.astype(dst_ref.dtype)

    def store_padded(dst_ref, z):
        swapped = pltpu.roll(z, HEAD_DIM, 1)
        zero = jnp.zeros_like(z)
        parts = (jnp.where(low, z, zero), jnp.where(low, zero, swapped),
                 jnp.where(low, swapped, zero), jnp.where(low, zero, z))
        for i, part in enumerate(parts):
            dst_ref[:, i * LANES:(i + 1) * LANES] = part.astype(dst_ref.dtype)

    def store_padded_transposed(dst_ref, z):
        zt = z.T
        swapped = jnp.concatenate([zt[HEAD_DIM:], zt[:HEAD_DIM]], axis=0)
        top = lax.broadcasted_iota(jnp.int32, zt.shape, 0) < HEAD_DIM
        zero = jnp.zeros_like(zt)
        parts = (jnp.where(top, zt, zero), jnp.where(top, zero, swapped),
                 jnp.where(top, swapped, zero), jnp.where(top, zero, zt))
        for i, part in enumerate(parts):
            dst_ref[i * LANES:(i + 1) * LANES, :] = part.astype(dst_ref.dtype)

    store_rope(qa_ref, project(0, 512), QK_SCALE * LOG2E)
    store_rope(ka_ref, project(512, 512), 1.0)
    store_rope(qb_ref, project(1536, 512), QK_SCALE * LOG2E)
    zkv = project(2048, 2 * LANES)
    store_padded(kb_ref, _rope_block(zkv[:, :LANES], cos, sin_signed, first_half))
    zva = project(1024, 512)
    zvb = zkv[:, LANES:]
    if v_transposed:
        for i in range(zva.shape[1] // LANES):
            va_ref[i * LANES:(i + 1) * LANES, :] = zva[:, i * LANES:(i + 1) * LANES].T.astype(va_ref.dtype)
        store_padded_transposed(vb_ref, zvb)
    else:
        va_ref[...] = zva.astype(va_ref.dtype)
        store_padded(vb_ref, zvb)


def _const_spec(shape):
    zeros = (0,) * len(shape)
    return pl.BlockSpec(shape, lambda *_: zeros, pipeline_mode=pl.Buffered(1))


def _ffn_in(x, cos, sin_signed, n1, wg, wu, wd, n2, win, row_tile, v_transposed):
    rows, d = x.shape
    pos_tiles = cos.shape[0] // row_tile
    row = lambda i: (i, 0)
    out_w = 512
    wide = pl.BlockSpec((row_tile, out_w), row)
    wide_shape = jax.ShapeDtypeStruct((rows, out_w), BF16)
    v_spec = pl.BlockSpec((out_w, row_tile), lambda i: (0, i)) if v_transposed else wide
    v_shape = jax.ShapeDtypeStruct((out_w, rows), BF16) if v_transposed else wide_shape
    return pl.pallas_call(
        functools.partial(_ffn_in_kernel, v_transposed=v_transposed),
        grid=(rows // row_tile,),
        in_specs=[
            pl.BlockSpec((row_tile, d), row),
            pl.BlockSpec((row_tile, LANES), lambda i: (i % pos_tiles, 0)),
            pl.BlockSpec((row_tile, LANES), lambda i: (i % pos_tiles, 0)),
            _const_spec(n1.shape), _const_spec(wg.shape), _const_spec(wu.shape), _const_spec(wd.shape),
            _const_spec(n2.shape), _const_spec(win.shape),
        ],
        out_specs=[pl.BlockSpec((row_tile, d), row), wide, wide, v_spec, wide, wide, v_spec],
        out_shape=[jax.ShapeDtypeStruct((rows, d), F32), wide_shape, wide_shape, v_shape, wide_shape, wide_shape,
                   v_shape],
        compiler_params=pltpu.CompilerParams(dimension_semantics=("parallel",),
                                             vmem_limit_bytes=VMEM_LIMIT_ROWWISE),
        name="ffn_in",
    )(x, cos, sin_signed, n1, wg, wu, wd, n2, win)


def _diff_attn_kernel(lq1_ref, lk1_ref, lq2_ref, lk2_ref, gain_ref, q_ref, km_ref, vmt_ref, k_ref, vt_ref,
                      o_ref, qz_sc, sa_sc, sb_sc, m_sc, acc_sc):
    lam = (jnp.exp(jnp.sum(lq1_ref[...] * lk1_ref[...], keepdims=True))
           - jnp.exp(jnp.sum(lq2_ref[...] * lk2_ref[...], keepdims=True)) + LAMBDA_INIT)
    tq = sa_sc.shape[2]
    tk = sa_sc.shape[1]
    vdim = vt_ref.shape[0]
    nq = q_ref.shape[0] // tq
    nk = vt_ref.shape[1] // tk

    def with_ones(vt):
        return jnp.concatenate([vt, jnp.ones((BF16_SUBLANES, vt.shape[1]), BF16)], axis=0)

    def load_q(i, slot):
        q = q_ref[pl.ds(pl.multiple_of(i * tq, tq), tq), :]
        lane = lax.broadcasted_iota(jnp.int32, q.shape, 1)
        zero = jnp.zeros_like(q)
        qz_sc[slot, 0] = jnp.where(lane < HEAD_DIM, q, zero)
        qz_sc[slot, 1] = jnp.where(lane < HEAD_DIM, zero, q)

    def produce(t, s_sc, slot):
        kt = k_ref[pl.ds(pl.multiple_of(t * tk, tk), tk), :]
        maxima = []
        for c in range(2):
            s = lax.dot_general(kt, qz_sc[slot, c], _NT, preferred_element_type=F32)
            s_sc[c] = s
            maxima.append(jnp.max(s, axis=0, keepdims=True))
        return tuple(maxima)

    def fold(c, s, s_max, vt):
        m_prev = m_sc[c]
        m_new = jnp.maximum(m_prev, s_max)
        alpha = jnp.exp2(m_prev - m_new)
        p = jnp.exp2(s - m_new)
        acc_sc[c] = alpha * acc_sc[c] + jnp.dot(vt, p.astype(BF16), preferred_element_type=F32)
        m_sc[c] = m_new

    def consume(t, s_sc, maxima):
        vt = with_ones(vt_ref[:, pl.ds(pl.multiple_of(t * tk, tk), tk)])
        for c in range(2):
            fold(c, s_sc[c], maxima[c], vt)

    def consume_meta(slot):
        vmt = with_ones(vmt_ref[...])
        for c in range(2):
            s = lax.dot_general(km_ref[...], qz_sc[slot, c], _NT, preferred_element_type=F32)
            fold(c, s, jnp.max(s, axis=0, keepdims=True), vmt)

    def finalize(i):
        def normalised(c):
            acc = acc_sc[c]
            return acc[:vdim] * (1.0 / acc[vdim:vdim + 1])

        a = normalised(0) - lam * normalised(1)
        ms = jnp.mean(a * a, axis=0, keepdims=True)
        y = a * lax.rsqrt(ms + EPS) * gain_ref[...] * (1.0 - LAMBDA_INIT)
        o_ref[pl.ds(pl.multiple_of(i * tq, tq), tq), :] = y.T.astype(o_ref.dtype)

    def query_tile(i, maxima_a):
        slot = i % 2
        for c in range(2):
            m_sc[c] = jnp.full(m_sc.shape[1:], NEG, F32)
            acc_sc[c] = jnp.zeros(acc_sc.shape[1:], F32)

        def pair(j, maxima_a):
            t = 2 * j
            maxima_b = produce(t + 1, sb_sc, slot)
            consume(t, sa_sc, maxima_a)
            maxima_a = produce(t + 2, sa_sc, slot)
            consume(t + 1, sb_sc, maxima_b)
            return maxima_a

        maxima_a = lax.fori_loop(0, nk // 2 - 1, pair, maxima_a)
        consume_meta(slot)
        maxima_b = produce(nk - 1, sb_sc, slot)
        consume(nk - 2, sa_sc, maxima_a)
        load_q(jnp.minimum(i + 1, nq - 1), 1 - slot)
        maxima_next = produce(0, sa_sc, 1 - slot)
        consume(nk - 1, sb_sc, maxima_b)
        finalize(i)
        return maxima_next

    load_q(0, 0)
    lax.fori_loop(0, nq, query_tile, produce(0, sa_sc, 0))


def _diff_attn(lams, gain_col, qa, ka_meta, vat_meta, ka, vat, batch):
    rows = qa.shape[0]
    seq = rows // batch
    vdim = 2 * HEAD_DIM
    tq, tk = DIFF_TQ, DIFF_TK
    assert (seq // tk) % 2 == 0
    lam_spec = pl.BlockSpec((1, HEAD_DIM), lambda b, h: (0, 0))
    return pl.pallas_call(
        _diff_attn_kernel,
        grid=(batch, DIFF_HEADS),
        in_specs=[
            lam_spec, lam_spec, lam_spec, lam_spec,
            pl.BlockSpec((vdim, 1), lambda b, h: (0, 0)),
            pl.BlockSpec((seq, LANES), lambda b, h: (b, h)),
            pl.BlockSpec((N_META, LANES), lambda b, h: (0, h)),
            pl.BlockSpec((vdim, N_META), lambda b, h: (h, 0)),
            pl.BlockSpec((seq, LANES), lambda b, h: (b, h)),
            pl.BlockSpec((vdim, seq), lambda b, h: (h, b)),
        ],
        out_specs=pl.BlockSpec((seq, LANES), lambda b, h: (b, h)),
        out_shape=jax.ShapeDtypeStruct((rows, DIFF_HEADS * vdim), BF16),
        scratch_shapes=[pltpu.VMEM((2, 2, tq, LANES), BF16),
                        pltpu.VMEM((2, tk, tq), F32), pltpu.VMEM((2, tk, tq), F32),
                        pltpu.VMEM((2, 1, tq), F32),
                        pltpu.VMEM((2, vdim + BF16_SUBLANES, tq), F32)],
        compiler_params=pltpu.CompilerParams(dimension_semantics=("parallel", "parallel"),
                                             vmem_limit_bytes=VMEM_LIMIT_ATTN),
        name="diff_attn",
    )(*lams, gain_col, qa, ka_meta, vat_meta, ka, vat)


def _win_attn_kernel(sink_ref, q_ref, km_ref, vmt_ref, k_ref, vt_ref, o_ref,
                     delta_sc, sa_sc, sb_sc, sma_sc, smb_sc):
    g = pl.program_id(1)
    blk = pl.program_id(2)
    seq = k_ref.shape[0]
    tq, span = WIN_TQ, WIN_SPAN
    n = 2 * tq
    nt = q_ref.shape[0] // tq
    delta_sc[...] = (lax.broadcasted_iota(jnp.int32, (span, tq), 0)
                     - lax.broadcasted_iota(jnp.int32, (span, tq), 1))
    first_pair = lax.broadcasted_iota(jnp.int32, (1, n), 1) < tq
    sinks = [jnp.where(first_pair, sink_ref[g * WIN_GROUP + par], sink_ref[g * WIN_GROUP + 2 + par]) * LOG2E
             for par in range(2)]

    def key_start(i):
        t0 = blk * (nt * tq) + i * tq
        return t0, pl.multiple_of(jnp.clip(t0 - WINDOW, 0, seq - span), WINDOW)

    def produce(i, s_sc, sm_sc):
        t0, ks = key_start(i)
        off = ks - t0
        delta = delta_sc[...]
        valid = (delta >= -WINDOW - off) & (delta <= WINDOW - off)
        rows = pl.ds(pl.multiple_of(i * tq, tq), tq)
        q_cat = jnp.concatenate([q_ref[rows, :LANES], q_ref[rows, LANES:]], axis=0)
        maxima = []
        for par in range(2):
            part = slice(par * LANES, (par + 1) * LANES)
            s = lax.dot_general(k_ref[pl.ds(ks, span), part], q_cat, _NT, preferred_element_type=F32)
            s = jnp.concatenate([jnp.where(valid, s[:, :tq], NEG), jnp.where(valid, s[:, tq:], NEG)], axis=1)
            sm = lax.dot_general(km_ref[:, part], q_cat, _NT, preferred_element_type=F32)
            s_sc[par] = s
            sm_sc[par] = sm
            maxima.append(jnp.maximum(jnp.maximum(jnp.max(s, axis=0, keepdims=True),
                                                  jnp.max(sm, axis=0, keepdims=True)), sinks[par]))
        return tuple(maxima)

    def consume(i, s_sc, sm_sc, maxima):
        _, ks = key_start(i)
        o_t = None
        for par in range(2):
            part = slice(par * LANES, (par + 1) * LANES)
            m = maxima[par]
            p = jnp.exp2(s_sc[par] - m)
            pm = jnp.exp2(sm_sc[par] - m)
            l = jnp.sum(p, axis=0, keepdims=True) + jnp.sum(pm, axis=0, keepdims=True) + jnp.exp2(sinks[par] - m)
            o = (jnp.dot(vt_ref[part, pl.ds(ks, span)], p.astype(BF16), preferred_element_type=F32)
                 + jnp.dot(vmt_ref[part, :], pm.astype(BF16), preferred_element_type=F32)) * (1.0 / l)
            o_t = o if o_t is None else o_t + o
        rows = pl.ds(pl.multiple_of(i * tq, tq), tq)
        for pair in range(WIN_GROUP // 2):
            o_ref[rows, pair * LANES:(pair + 1) * LANES] = o_t[:, pair * tq:(pair + 1) * tq].T.astype(o_ref.dtype)

    def pair_of_tiles(j, maxima_a):
        i = 2 * j
        maxima_b = produce(i + 1, sb_sc, smb_sc)
        consume(i, sa_sc, sma_sc, maxima_a)
        maxima_a = produce(i + 2, sa_sc, sma_sc)
        consume(i + 1, sb_sc, smb_sc, maxima_b)
        return maxima_a

    maxima_a = lax.fori_loop(0, nt // 2 - 1, pair_of_tiles, produce(0, sa_sc, sma_sc), unroll=5)
    maxima_b = produce(nt - 1, sb_sc, smb_sc)
    consume(nt - 2, sa_sc, sma_sc, maxima_a)
    consume(nt - 1, sb_sc, smb_sc, maxima_b)


def _win_attn(sink, qb, kb_meta, vbt_meta, kb, vbt, batch):
    rows = qb.shape[0]
    seq = rows // batch
    nblk = seq // WIN_QBLOCK
    assert (WIN_QBLOCK // WIN_TQ) % 2 == 0
    width = 2 * LANES
    n = 2 * WIN_TQ
    return pl.pallas_call(
        _win_attn_kernel,
        grid=(batch, WIN_KV_HEADS, nblk),
        in_specs=[
            pl.BlockSpec(memory_space=pltpu.SMEM),
            pl.BlockSpec((WIN_QBLOCK, width), lambda b, g, i: (b * nblk + i, g)),
            pl.BlockSpec((N_META, width), lambda b, g, i: (0, g)),
            pl.BlockSpec((width, N_META), lambda b, g, i: (g, 0)),
            pl.BlockSpec((seq, width), lambda b, g, i: (b, g)),
            pl.BlockSpec((width, seq), lambda b, g, i: (g, b)),
        ],
        out_specs=pl.BlockSpec((WIN_QBLOCK, width), lambda b, g, i: (b * nblk + i, g)),
        out_shape=jax.ShapeDtypeStruct((rows, WIN_KV_HEADS * width), BF16),
        scratch_shapes=[pltpu.VMEM((WIN_SPAN, WIN_TQ), jnp.int32),
                        pltpu.VMEM((2, WIN_SPAN, n), F32), pltpu.VMEM((2, WIN_SPAN, n), F32),
                        pltpu.VMEM((2, N_META, n), F32), pltpu.VMEM((2, N_META, n), F32)],
        compiler_params=pltpu.CompilerParams(dimension_semantics=("parallel", "parallel", "arbitrary"),
                                             vmem_limit_bytes=VMEM_LIMIT_ATTN),
        name="win_attn",
    )(sink, qb, kb_meta, vbt_meta, kb, vbt)


def _out_ffn_kernel(h_ref, oa_ref, ob_ref, wn_ref, wo_ref, n_ref, wg_ref, wu_ref, wd_ref, fn_ref, out_ref):
    half = oa_ref.shape[1]
    ob = _rms(ob_ref[...].astype(F32), wn_ref[...]).astype(BF16)
    mix = (jnp.dot(oa_ref[...], wo_ref[:half, :], preferred_element_type=F32)
           + jnp.dot(ob, wo_ref[half:, :], preferred_element_type=F32))
    h = h_ref[...] + mix
    xn = _rms(h, n_ref[...]).astype(BF16)
    h = h + 0.5 * _swiglu(xn, wg_ref, wu_ref, wd_ref)
    out_ref[...] = _rms(h, fn_ref[...])


def _out_ffn(h1, oa, ob, wn, wo, n, wg, wu, wd, fn):
    rows, d = h1.shape
    half = oa.shape[1]
    row = lambda i: (i, 0)
    return pl.pallas_call(
        _out_ffn_kernel,
        grid=(rows // ROW_TILE,),
        in_specs=[
            pl.BlockSpec((ROW_TILE, d), row), pl.BlockSpec((ROW_TILE, half), row), pl.BlockSpec((ROW_TILE, half), row),
            _const_spec(wn.shape), _const_spec(wo.shape), _const_spec(n.shape),
            _const_spec(wg.shape), _const_spec(wu.shape), _const_spec(wd.shape), _const_spec(fn.shape),
        ],
        out_specs=pl.BlockSpec((ROW_TILE, d), row),
        out_shape=jax.ShapeDtypeStruct((rows, d), F32),
        compiler_params=pltpu.CompilerParams(dimension_semantics=("parallel",),
                                             vmem_limit_bytes=VMEM_LIMIT_ROWWISE),
        name="out_ffn",
    )(h1, oa, ob, wn, wo, n, wg, wu, wd, fn)


def _rope_tables(length):
    pos = jnp.arange(length, dtype=F32)
    inv = ROPE_THETA ** (-jnp.arange(0, HEAD_DIM, 2, dtype=F32) / HEAD_DIM)
    ang = pos[:, None] * inv[None, :]
    cos, sin = jnp.cos(ang), jnp.sin(ang)
    return jnp.tile(cos, (1, 4)), jnp.tile(jnp.concatenate([-sin, sin], axis=1), (1, 2))


def kernel(x, meta_tokens, ffn1_norm, ffn1_w_gate, ffn1_w_up, ffn1_w_down, mix_norm, w_in, lambda_q1, lambda_k1, lambda_q2, lambda_k2, diff_norm, win_sink, win_norm, w_out, ffn2_norm, ffn2_w_gate, ffn2_w_up, ffn2_w_down, final_norm):
    batch, seq, d = x.shape
    assert ffn1_norm.shape[0] == 1, "single layer only"
    assert seq % ROW_TILE == 0 and seq % DIFF_TQ == 0 and seq % DIFF_TK == 0 and seq % WIN_QBLOCK == 0

    cos, sin_signed = _rope_tables(N_META + seq)
    ffn_in_weights = (ffn1_norm, ffn1_w_gate[0].astype(BF16), ffn1_w_up[0].astype(BF16),
                      ffn1_w_down[0].astype(BF16), mix_norm, w_in[0].astype(BF16))

    meta = _ffn_in(meta_tokens.astype(x.dtype), cos[:N_META], sin_signed[:N_META], *ffn_in_weights, N_META,
                   v_transposed=False)
    real = _ffn_in(x.reshape(batch * seq, d), cos[N_META:], sin_signed[N_META:], *ffn_in_weights, ROW_TILE,
                   v_transposed=True)
    _, _, ka_m, va_m, _, kb_m, vb_m = meta
    h1, qa, ka, vat, qb, kb, vbt = real

    lams = (lambda_q1, lambda_k1, lambda_q2, lambda_k2)
    out_a = _diff_attn(lams, diff_norm.reshape(2 * HEAD_DIM, 1), qa, ka_m, va_m.T, ka, vat, batch)

    out_b = _win_attn(win_sink.reshape(-1), qb, kb_m, vb_m.T, kb, vbt, batch)

    out = _out_ffn(h1, out_a, out_b, win_norm, w_out[0].astype(BF16), ffn2_norm,
                   ffn2_w_gate[0].astype(BF16), ffn2_w_up[0].astype(BF16), ffn2_w_down[0].astype(BF16),
                   final_norm.reshape(1, d))
    return out.reshape(batch, seq, d)
```

```python
import functools
import math

import jax
import jax.numpy as jnp
from jax import lax
from jax.experimental import pallas as pl
from jax.experimental.pallas import tpu as pltpu

N_META = 16
HEAD_DIM = 64
DIFF_HEADS = 4
WIN_KV_HEADS = 2
WIN_GROUP = 4
WINDOW = 128
ROPE_THETA = 10000.0
EPS = 1e-6
NEG = -1e30
LAMBDA_INIT = 0.8 - 0.6 * math.exp(-0.3 * 0)
QK_SCALE = HEAD_DIM ** -0.5
LOG2E = math.log2(math.e)

LANES = 128
BF16_SUBLANES = 16
VMEM_LIMIT_ROWWISE = 56 * 1024 * 1024
VMEM_LIMIT_ATTN = 40 * 1024 * 1024

ROW_TILE = 512
MXU_WIDTH = 256
FF_CHUNK = 6 * MXU_WIDTH
DIFF_TQ = 512
DIFF_TK = 256
DIFF_UNROLL = 5
WIN_TQ = 128
WIN_SPAN = WIN_TQ + 2 * WINDOW
WIN_QBLOCK = 4096

F32 = jnp.float32
BF16 = jnp.bfloat16
_NT = (((1,), (1,)), ((), ()))


def _rms(x, gain):
    return x * lax.rsqrt(jnp.mean(x * x, axis=-1, keepdims=True) + EPS) * gain


def _swiglu(xn, wg_ref, wu_ref, wd_ref):
    d_ff = wg_ref.shape[1]
    acc = None
    for start in range(0, d_ff, FF_CHUNK):
        sl = slice(start, min(start + FF_CHUNK, d_ff))
        g = jnp.dot(xn, wg_ref[:, sl], preferred_element_type=F32)
        u = jnp.dot(xn, wu_ref[:, sl], preferred_element_type=F32)
        a = (g * jax.nn.sigmoid(g) * u).astype(BF16)
        d = jnp.dot(a, wd_ref[sl, :], preferred_element_type=F32)
        acc = d if acc is None else acc + d
    return acc


def _rope_block(x, cos, sin_signed, first_half):
    partner = jnp.where(first_half, pltpu.roll(x, LANES - 32, 1), pltpu.roll(x, 32, 1))
    return x * cos + partner * sin_signed


def _ffn_in_kernel(x_ref, cos_ref, sin_ref, n1_ref, wg_ref, wu_ref, wd_ref, n2_ref, win_ref,
                   h_ref, qa_ref, ka_ref, va_ref, qb_ref, kb_ref, vb_ref, *, v_transposed):
    x = x_ref[...]
    xn = _rms(x, n1_ref[...]).astype(BF16)
    h = x + 0.5 * _swiglu(xn, wg_ref, wu_ref, wd_ref)
    h_ref[...] = h
    u = _rms(h, n2_ref[...]).astype(BF16)

    rows = x.shape[0]
    cos = cos_ref[...]
    sin_signed = sin_ref[...]
    lane = lax.broadcasted_iota(jnp.int32, (rows, LANES), 1)
    first_half = (lane % HEAD_DIM) < (HEAD_DIM // 2)
    low = lane < HEAD_DIM

    def project(col0, width):
        return jnp.dot(u, win_ref[:, col0:col0 + width], preferred_element_type=F32)

    def store_rope(dst_ref, z, scale):
        for i in range(z.shape[1] // LANES):
            blk = _rope_block(z[:, i * LANES:(i + 1) * LANES], cos, sin_signed, first_half)
            if scale != 1.0:
                blk = blk * scale
            dst_ref[:, i * LANES:(i + 1) * LANES] = blk.astype(dst_ref.dtype)

    def store_padded(dst_ref, z):
        swapped = pltpu.roll(z, HEAD_DIM, 1)
        zero = jnp.zeros_like(z)
        parts = (jnp.where(low, z, zero), jnp.where(low, zero, swapped),
                 jnp.where(low, swapped, zero), jnp.where(low, zero, z))
        for i, part in enumerate(parts):
            dst_ref[:, i * LANES:(i + 1) * LANES] = part.astype(dst_ref.dtype)

    def store_padded_transposed(dst_ref, z):
        zt = z.T
        swapped = jnp.concatenate([zt[HEAD_DIM:], zt[:HEAD_DIM]], axis=0)
        top = lax.broadcasted_iota(jnp.int32, zt.shape, 0) < HEAD_DIM
        zero = jnp.zeros_like(zt)
        parts = (jnp.where(top, zt, zero), jnp.where(top, zero, swapped),
                 jnp.where(top, swapped, zero), jnp.where(top, zero, zt))
        for i, part in enumerate(parts):
            dst_ref[i * LANES:(i + 1) * LANES, :] = part.astype(dst_ref.dtype)

    store_rope(qa_ref, project(0, 512), QK_SCALE * LOG2E)
    store_rope(ka_ref, project(512, 512), 1.0)
    store_rope(qb_ref, project(1536, 512), QK_SCALE * LOG2E)
    zkv = project(2048, 2 * LANES)
    store_padded(kb_ref, _rope_block(zkv[:, :LANES], cos, sin_signed, first_half))
    zva = project(1024, 512)
    zvb = zkv[:, LANES:]
    if v_transposed:
        for i in range(zva.shape[1] // LANES):
            va_ref[i * LANES:(i + 1) * LANES, :] = zva[:, i * LANES:(i + 1) * LANES].T.astype(va_ref.dtype)
        store_padded_transposed(vb_ref, zvb)
    else:
        va_ref[...] = zva.astype(va_ref.dtype)
        store_padded(vb_ref, zvb)


def _const_spec(shape):
    zeros = (0,) * len(shape)
    return pl.BlockSpec(shape, lambda *_: zeros, pipeline_mode=pl.Buffered(1))


def _ffn_in(x, cos, sin_signed, n1, wg, wu, wd, n2, win, row_tile, v_transposed):
    rows, d = x.shape
    pos_tiles = cos.shape[0] // row_tile
    row = lambda i: (i, 0)
    out_w = 512
    wide = pl.BlockSpec((row_tile, out_w), row)
    wide_shape = jax.ShapeDtypeStruct((rows, out_w), BF16)
    v_spec = pl.BlockSpec((out_w, row_tile), lambda i: (0, i)) if v_transposed else wide
    v_shape = jax.ShapeDtypeStruct((out_w, rows), BF16) if v_transposed else wide_shape
    return pl.pallas_call(
        functools.partial(_ffn_in_kernel, v_transposed=v_transposed),
        grid=(rows // row_tile,),
        in_specs=[
            pl.BlockSpec((row_tile, d), row),
            pl.BlockSpec((row_tile, LANES), lambda i: (i % pos_tiles, 0)),
            pl.BlockSpec((row_tile, LANES), lambda i: (i % pos_tiles, 0)),
            _const_spec(n1.shape), _const_spec(wg.shape), _const_spec(wu.shape), _const_spec(wd.shape),
            _const_spec(n2.shape), _const_spec(win.shape),
        ],
        out_specs=[pl.BlockSpec((row_tile, d), row), wide, wide, v_spec, wide, wide, v_spec],
        out_shape=[jax.ShapeDtypeStruct((rows, d), F32), wide_shape, wide_shape, v_shape, wide_shape, wide_shape,
                   v_shape],
        compiler_params=pltpu.CompilerParams(dimension_semantics=("parallel",),
                                             vmem_limit_bytes=VMEM_LIMIT_ROWWISE),
        name="ffn_in",
    )(x, cos, sin_signed, n1, wg, wu, wd, n2, win)


def _diff_attn_kernel(lq1_ref, lk1_ref, lq2_ref, lk2_ref, gain_ref, q_ref, km_ref, vmt_ref, k_ref, vt_ref,
                      o_ref, qz_sc, sa_sc, sb_sc, sm_sc, m_sc, acc_sc):
    lam = (jnp.exp(jnp.sum(lq1_ref[...] * lk1_ref[...], keepdims=True))
           - jnp.exp(jnp.sum(lq2_ref[...] * lk2_ref[...], keepdims=True)) + LAMBDA_INIT)
    tq = sa_sc.shape[2]
    tk = sa_sc.shape[1]
    vdim = vt_ref.shape[0]
    nq = q_ref.shape[0] // tq
    nk = vt_ref.shape[1] // tk

    def with_ones(vt):
        return jnp.concatenate([vt, jnp.ones((BF16_SUBLANES, vt.shape[1]), BF16)], axis=0)

    def load_q(i, slot):
        q = q_ref[pl.ds(pl.multiple_of(i * tq, tq), tq), :]
        lane = lax.broadcasted_iota(jnp.int32, q.shape, 1)
        zero = jnp.zeros_like(q)
        qz_sc[slot, 0] = jnp.where(lane < HEAD_DIM, q, zero)
        qz_sc[slot, 1] = jnp.where(lane < HEAD_DIM, zero, q)

    def produce(t, s_sc, slot):
        kt = k_ref[pl.ds(pl.multiple_of(t * tk, tk), tk), :]
        maxima = []
        for c in range(2):
            s = lax.dot_general(kt, qz_sc[slot, c], _NT, preferred_element_type=F32)
            s_sc[c] = s
            maxima.append(jnp.max(s, axis=0, keepdims=True))
        return tuple(maxima)

    def fold(c, s_ref, s_max, vt):
        m_prev = m_sc[c]
        m_new = jnp.maximum(m_prev, s_max)
        alpha = jnp.exp2(m_prev - m_new)
        m_sc[c] = m_new
        for n0 in range(0, tq, MXU_WIDTH):
            cols = slice(n0, n0 + MXU_WIDTH)
            p = jnp.exp2(s_ref[:, cols] - m_new[:, cols])
            acc_sc[c, :, cols] = (alpha[:, cols] * acc_sc[c, :, cols]
                                  + jnp.dot(vt, p.astype(BF16), preferred_element_type=F32))

    def consume(t, s_sc, maxima):
        vt = with_ones(vt_ref[:, pl.ds(pl.multiple_of(t * tk, tk), tk)])
        for c in range(2):
            fold(c, s_sc.at[c], maxima[c], vt)

    def meta_scores(slot):
        for c in range(2):
            sm_sc[c] = lax.dot_general(km_ref[...], qz_sc[slot, c], _NT, preferred_element_type=F32)

    def consume_meta():
        vmt = with_ones(vmt_ref[...])
        for c in range(2):
            fold(c, sm_sc.at[c], jnp.max(sm_sc[c], axis=0, keepdims=True), vmt)

    def finalize(i):
        def normalised(c):
            acc = acc_sc[c]
            return acc[:vdim] * (1.0 / acc[vdim:vdim + 1])

        a = normalised(0) - lam * normalised(1)
        ms = jnp.mean(a * a, axis=0, keepdims=True)
        y = a * lax.rsqrt(ms + EPS) * gain_ref[...] * (1.0 - LAMBDA_INIT)
        o_ref[pl.ds(pl.multiple_of(i * tq, tq), tq), :] = y.T.astype(o_ref.dtype)

    def query_tile(i, maxima_a):
        slot = i % 2
        for c in range(2):
            m_sc[c] = jnp.full(m_sc.shape[1:], NEG, F32)
            acc_sc[c] = jnp.zeros(acc_sc.shape[1:], F32)

        def pair(j, maxima_a):
            t = 2 * j
            maxima_b = produce(t + 1, sb_sc, slot)
            consume(t, sa_sc, maxima_a)
            maxima_a = produce(t + 2, sa_sc, slot)
            consume(t + 1, sb_sc, maxima_b)
            return maxima_a

        maxima_a = lax.fori_loop(0, nk // 2 - 1, pair, maxima_a, unroll=DIFF_UNROLL)
        meta_scores(slot)
        maxima_b = produce(nk - 1, sb_sc, slot)
        consume(nk - 2, sa_sc, maxima_a)
        load_q(jnp.minimum(i + 1, nq - 1), 1 - slot)
        maxima_next = produce(0, sa_sc, 1 - slot)
        consume_meta()
        consume(nk - 1, sb_sc, maxima_b)
        finalize(i)
        return maxima_next

    load_q(0, 0)
    lax.fori_loop(0, nq, query_tile, produce(0, sa_sc, 0))


def _diff_attn(lams, gain_col, qa, ka_meta, vat_meta, ka, vat, batch):
    rows = qa.shape[0]
    seq = rows // batch
    vdim = 2 * HEAD_DIM
    tq, tk = DIFF_TQ, DIFF_TK
    assert (seq // tk) % 2 == 0
    lam_spec = pl.BlockSpec((1, HEAD_DIM), lambda b, h: (0, 0))
    return pl.pallas_call(
        _diff_attn_kernel,
        grid=(batch, DIFF_HEADS),
        in_specs=[
            lam_spec, lam_spec, lam_spec, lam_spec,
            pl.BlockSpec((vdim, 1), lambda b, h: (0, 0)),
            pl.BlockSpec((seq, LANES), lambda b, h: (b, h)),
            pl.BlockSpec((N_META, LANES), lambda b, h: (0, h)),
            pl.BlockSpec((vdim, N_META), lambda b, h: (h, 0)),
            pl.BlockSpec((seq, LANES), lambda b, h: (b, h)),
            pl.BlockSpec((vdim, seq), lambda b, h: (h, b)),
        ],
        out_specs=pl.BlockSpec((seq, LANES), lambda b, h: (b, h)),
        out_shape=jax.ShapeDtypeStruct((rows, DIFF_HEADS * vdim), BF16),
        scratch_shapes=[pltpu.VMEM((2, 2, tq, LANES), BF16),
                        pltpu.VMEM((2, tk, tq), F32), pltpu.VMEM((2, tk, tq), F32),
                        pltpu.VMEM((2, N_META, tq), F32),
                        pltpu.VMEM((2, 1, tq), F32),
                        pltpu.VMEM((2, vdim + BF16_SUBLANES, tq), F32)],
        compiler_params=pltpu.CompilerParams(dimension_semantics=("parallel", "parallel"),
                                             vmem_limit_bytes=VMEM_LIMIT_ATTN),
        name="diff_attn",
    )(*lams, gain_col, qa, ka_meta, vat_meta, ka, vat)


def _win_attn_kernel(sink_ref, q_ref, km_ref, vmt_ref, k_ref, vt_ref, o_ref,
                     delta_sc, sa_sc, sb_sc, sma_sc, smb_sc):
    g = pl.program_id(1)
    blk = pl.program_id(2)
    seq = k_ref.shape[0]
    tq, span = WIN_TQ, WIN_SPAN
    n = 2 * tq
    nt = q_ref.shape[0] // tq
    delta_sc[...] = (lax.broadcasted_iota(jnp.int32, (span, tq), 0)
                     - lax.broadcasted_iota(jnp.int32, (span, tq), 1))
    first_pair = lax.broadcasted_iota(jnp.int32, (1, n), 1) < tq
    sinks = [jnp.where(first_pair, sink_ref[g * WIN_GROUP + par], sink_ref[g * WIN_GROUP + 2 + par]) * LOG2E
             for par in range(2)]

    def key_start(i):
        t0 = blk * (nt * tq) + i * tq
        return t0, pl.multiple_of(jnp.clip(t0 - WINDOW, 0, seq - span), WINDOW)

    def produce(i, s_sc, sm_sc):
        t0, ks = key_start(i)
        off = ks - t0
        delta = delta_sc[...]
        valid = (delta >= -WINDOW - off) & (delta <= WINDOW - off)
        rows = pl.ds(pl.multiple_of(i * tq, tq), tq)
        q_cat = jnp.concatenate([q_ref[rows, :LANES], q_ref[rows, LANES:]], axis=0)
        maxima = []
        for par in range(2):
            part = slice(par * LANES, (par + 1) * LANES)
            s = lax.dot_general(k_ref[pl.ds(ks, span), part], q_cat, _NT, preferred_element_type=F32)
            s = jnp.concatenate([jnp.where(valid, s[:, :tq], NEG), jnp.where(valid, s[:, tq:], NEG)], axis=1)
            sm = lax.dot_general(km_ref[:, part], q_cat, _NT, preferred_element_type=F32)
            s_sc[par] = s
            sm_sc[par] = sm
            maxima.append(jnp.maximum(jnp.maximum(jnp.max(s, axis=0, keepdims=True),
                                                  jnp.max(sm, axis=0, keepdims=True)), sinks[par]))
        return tuple(maxima)

    def consume(i, s_sc, sm_sc, maxima):
        _, ks = key_start(i)
        o_t = None
        for par in range(2):
            part = slice(par * LANES, (par + 1) * LANES)
            m = maxima[par]
            p = jnp.exp2(s_sc[par] - m)
            pm = jnp.exp2(sm_sc[par] - m)
            l = jnp.sum(p, axis=0, keepdims=True) + jnp.sum(pm, axis=0, keepdims=True) + jnp.exp2(sinks[par] - m)
            o = (jnp.dot(vt_ref[part, pl.ds(ks, span)], p.astype(BF16), preferred_element_type=F32)
                 + jnp.dot(vmt_ref[part, :], pm.astype(BF16), preferred_element_type=F32)) * (1.0 / l)
            o_t = o if o_t is None else o_t + o
        rows = pl.ds(pl.multiple_of(i * tq, tq), tq)
        for pair in range(WIN_GROUP // 2):
            o_ref[rows, pair * LANES:(pair + 1) * LANES] = o_t[:, pair * tq:(pair + 1) * tq].T.astype(o_ref.dtype)

    def pair_of_tiles(j, maxima_a):
        i = 2 * j
        maxima_b = produce(i + 1, sb_sc, smb_sc)
        consume(i, sa_sc, sma_sc, maxima_a)
        maxima_a = produce(i + 2, sa_sc, sma_sc)
        consume(i + 1, sb_sc, smb_sc, maxima_b)
        return maxima_a

    maxima_a = lax.fori_loop(0, nt // 2 - 1, pair_of_tiles, produce(0, sa_sc, sma_sc), unroll=5)
    maxima_b = produce(nt - 1, sb_sc, smb_sc)
    consume(nt - 2, sa_sc, sma_sc, maxima_a)
    consume(nt - 1, sb_sc, smb_sc, maxima_b)


def _win_attn(sink, qb, kb_meta, vbt_meta, kb, vbt, batch):
    rows = qb.shape[0]
    seq = rows // batch
    nblk = seq // WIN_QBLOCK
    assert (WIN_QBLOCK // WIN_TQ) % 2 == 0
    width = 2 * LANES
    n = 2 * WIN_TQ
    return pl.pallas_call(
        _win_attn_kernel,
        grid=(batch, WIN_KV_HEADS, nblk),
        in_specs=[
            pl.BlockSpec(memory_space=pltpu.SMEM),
            pl.BlockSpec((WIN_QBLOCK, width), lambda b, g, i: (b * nblk + i, g)),
            pl.BlockSpec((N_META, width), lambda b, g, i: (0, g)),
            pl.BlockSpec((width, N_META), lambda b, g, i: (g, 0)),
            pl.BlockSpec((seq, width), lambda b, g, i: (b, g)),
            pl.BlockSpec((width, seq), lambda b, g, i: (g, b)),
        ],
        out_specs=pl.BlockSpec((WIN_QBLOCK, width), lambda b, g, i: (b * nblk + i, g)),
        out_shape=jax.ShapeDtypeStruct((rows, WIN_KV_HEADS * width), BF16),
        scratch_shapes=[pltpu.VMEM((WIN_SPAN, WIN_TQ), jnp.int32),
                        pltpu.VMEM((2, WIN_SPAN, n), F32), pltpu.VMEM((2, WIN_SPAN, n), F32),
                        pltpu.VMEM((2, N_META, n), F32), pltpu.VMEM((2, N_META, n), F32)],
        compiler_params=pltpu.CompilerParams(dimension_semantics=("parallel", "parallel", "arbitrary"),
                                             vmem_limit_bytes=VMEM_LIMIT_ATTN),
        name="win_attn",
    )(sink, qb, kb_meta, vbt_meta, kb, vbt)


def _out_ffn_kernel(h_ref, oa_ref, ob_ref, wn_ref, wo_ref, n_ref, wg_ref, wu_ref, wd_ref, fn_ref, out_ref):
    half = oa_ref.shape[1]
    ob = _rms(ob_ref[...].astype(F32), wn_ref[...]).astype(BF16)
    mix = (jnp.dot(oa_ref[...], wo_ref[:half, :], preferred_element_type=F32)
           + jnp.dot(ob, wo_ref[half:, :], preferred_element_type=F32))
    h = h_ref[...] + mix
    xn = _rms(h, n_ref[...]).astype(BF16)
    h = h + 0.5 * _swiglu(xn, wg_ref, wu_ref, wd_ref)
    out_ref[...] = _rms(h, fn_ref[...])


def _out_ffn(h1, oa, ob, wn, wo, n, wg, wu, wd, fn):
    rows, d = h1.shape
    half = oa.shape[1]
    row = lambda i: (i, 0)
    return pl.pallas_call(
        _out_ffn_kernel,
        grid=(rows // ROW_TILE,),
        in_specs=[
            pl.BlockSpec((ROW_TILE, d), row), pl.BlockSpec((ROW_TILE, half), row), pl.BlockSpec((ROW_TILE, half), row),
            _const_spec(wn.shape), _const_spec(wo.shape), _const_spec(n.shape),
            _const_spec(wg.shape), _const_spec(wu.shape), _const_spec(wd.shape), _const_spec(fn.shape),
        ],
        out_specs=pl.BlockSpec((ROW_TILE, d), row),
        out_shape=jax.ShapeDtypeStruct((rows, d), F32),
        compiler_params=pltpu.CompilerParams(dimension_semantics=("parallel",),
                                             vmem_limit_bytes=VMEM_LIMIT_ROWWISE),
        name="out_ffn",
    )(h1, oa, ob, wn, wo, n, wg, wu, wd, fn)


def _rope_tables(length):
    pos = jnp.arange(length, dtype=F32)
    inv = ROPE_THETA ** (-jnp.arange(0, HEAD_DIM, 2, dtype=F32) / HEAD_DIM)
    ang = pos[:, None] * inv[None, :]
    cos, sin = jnp.cos(ang), jnp.sin(ang)
    return jnp.tile(cos, (1, 4)), jnp.tile(jnp.concatenate([-sin, sin], axis=1), (1, 2))


def kernel(x, meta_tokens, ffn1_norm, ffn1_w_gate, ffn1_w_up, ffn1_w_down, mix_norm, w_in, lambda_q1, lambda_k1, lambda_q2, lambda_k2, diff_norm, win_sink, win_norm, w_out, ffn2_norm, ffn2_w_gate, ffn2_w_up, ffn2_w_down, final_norm):
    batch, seq, d = x.shape
    assert ffn1_norm.shape[0] == 1, "single layer only"
    assert seq % ROW_TILE == 0 and seq % DIFF_TQ == 0 and seq % DIFF_TK == 0 and seq % WIN_QBLOCK == 0

    cos, sin_signed = _rope_tables(N_META + seq)
    ffn_in_weights = (ffn1_norm, ffn1_w_gate[0].astype(BF16), ffn1_w_up[0].astype(BF16),
                      ffn1_w_down[0].astype(BF16), mix_norm, w_in[0].astype(BF16))

    meta = _ffn_in(meta_tokens.astype(x.dtype), cos[:N_META], sin_signed[:N_META], *ffn_in_weights, N_META,
                   v_transposed=False)
    real = _ffn_in(x.reshape(batch * seq, d), cos[N_META:], sin_signed[N_META:], *ffn_in_weights, ROW_TILE,
                   v_transposed=True)
    _, _, ka_m, va_m, _, kb_m, vb_m = meta
    h1, qa, ka, vat, qb, kb, vbt = real

    lams = (lambda_q1, lambda_k1, lambda_q2, lambda_k2)
    out_a = _diff_attn(lams, diff_norm.reshape(2 * HEAD_DIM, 1), qa, ka_m, va_m.T, ka, vat, batch)

    out_b = _win_attn(win_sink.reshape(-1), qb, kb_m, vb_m.T, kb, vbt, batch)

    out = _out_ffn(h1, out_a, out_b, win_norm, w_out[0].astype(BF16), ffn2_norm,
                   ffn2_w_gate[0].astype(BF16), ffn2_w_up[0].astype(BF16), ffn2_w_down[0].astype(BF16),
                   final_norm.reshape(1, d))
    return out.reshape(batch, seq, d)
```

```python
import functools
import math

import jax
import jax.numpy as jnp
from jax import lax
from jax.experimental import pallas as pl
from jax.experimental.pallas import tpu as pltpu

N_META = 16
HEAD_DIM = 64
DIFF_HEADS = 4
WIN_KV_HEADS = 2
WIN_GROUP = 4
WINDOW = 128
ROPE_THETA = 10000.0
EPS = 1e-6
NEG = -1e30
LAMBDA_INIT = 0.8 - 0.6 * math.exp(-0.3 * 0)
QK_SCALE = HEAD_DIM ** -0.5
LOG2E = math.log2(math.e)

LANES = 128
BF16_SUBLANES = 16
VMEM_LIMIT_ROWWISE = 56 * 1024 * 1024
VMEM_LIMIT_ATTN = 40 * 1024 * 1024

ROW_TILE = 512
MXU_WIDTH = 256
FF_CHUNK = 6 * MXU_WIDTH
DIFF_TQ = 512
DIFF_TK = 1024
WIN_TQ = 128
WIN_SPAN = WIN_TQ + 2 * WINDOW
WIN_QBLOCK = 4096

F32 = jnp.float32
BF16 = jnp.bfloat16
_NT = (((1,), (1,)), ((), ()))


def _rms(x, gain):
    return x * lax.rsqrt(jnp.mean(x * x, axis=-1, keepdims=True) + EPS) * gain


def _rms_split(x, gain):
    return (x * gain).astype(BF16), lax.rsqrt(jnp.mean(x * x, axis=-1, keepdims=True) + EPS)


def _swiglu(xn, wg_ref, wu_ref, wd_ref, row_scale=None):
    d_ff = wg_ref.shape[1]
    acc = None
    for start in range(0, d_ff, FF_CHUNK):
        sl = slice(start, min(start + FF_CHUNK, d_ff))
        g = jnp.dot(xn, wg_ref[:, sl], preferred_element_type=F32)
        u = jnp.dot(xn, wu_ref[:, sl], preferred_element_type=F32)
        if row_scale is not None:
            g = g * row_scale
            u = u * row_scale
        a = (g * jax.nn.sigmoid(g) * u).astype(BF16)
        d = jnp.dot(a, wd_ref[sl, :], preferred_element_type=F32)
        acc = d if acc is None else acc + d
    return acc


def _rope_block(x, cos, sin_signed, first_half):
    partner = jnp.where(first_half, pltpu.roll(x, LANES - 32, 1), pltpu.roll(x, 32, 1))
    return x * cos + partner * sin_signed


def _ffn_in_kernel(x_ref, cos_ref, sin_ref, n1_ref, wg_ref, wu_ref, wd_ref, n2_ref, win_ref,
                   h_ref, qa_ref, ka_ref, va_ref, qb_ref, kb_ref, vb_ref, *, v_transposed):
    x = x_ref[...]
    xg, x_scale = _rms_split(x, n1_ref[...])
    h = x + 0.5 * _swiglu(xg, wg_ref, wu_ref, wd_ref, x_scale)
    h_ref[...] = h
    u, h_scale = _rms_split(h, n2_ref[...])

    rows = x.shape[0]
    cos = cos_ref[...]
    sin_signed = sin_ref[...]
    lane = lax.broadcasted_iota(jnp.int32, (rows, LANES), 1)
    first_half = (lane % HEAD_DIM) < (HEAD_DIM // 2)
    low = lane < HEAD_DIM

    def project(col0, width):
        return jnp.dot(u, win_ref[:, col0:col0 + width], preferred_element_type=F32) * h_scale

    def store_rope(dst_ref, z, scale):
        for i in range(z.shape[1] // LANES):
            blk = _rope_block(z[:, i * LANES:(i + 1) * LANES], cos, sin_signed, first_half)
            if scale != 1.0:
                blk = blk * scale
            dst_ref[:, i * LANES:(i + 1) * LANES] = blk.astype(dst_ref.dtype)

    def store_padded(dst_ref, z):
        swapped = pltpu.roll(z, HEAD_DIM, 1)
        zero = jnp.zeros_like(z)
        parts = (jnp.where(low, z, zero), jnp.where(low, zero, swapped),
                 jnp.where(low, swapped, zero), jnp.where(low, zero, z))
        for i, part in enumerate(parts):
            dst_ref[:, i * LANES:(i + 1) * LANES] = part.astype(dst_ref.dtype)

    def store_padded_transposed(dst_ref, z):
        zt = z.T
        swapped = jnp.concatenate([zt[HEAD_DIM:], zt[:HEAD_DIM]], axis=0)
        top = lax.broadcasted_iota(jnp.int32, zt.shape, 0) < HEAD_DIM
        zero = jnp.zeros_like(zt)
        parts = (jnp.where(top, zt, zero), jnp.where(top, zero, swapped),
                 jnp.where(top, swapped, zero), jnp.where(top, zero, zt))
        for i, part in enumerate(parts):
            dst_ref[i * LANES:(i + 1) * LANES, :] = part.astype(dst_ref.dtype)

    store_rope(qa_ref, project(0, 512), QK_SCALE * LOG2E)
    store_rope(ka_ref, project(512, 512), 1.0)
    store_rope(qb_ref, project(1536, 512), QK_SCALE * LOG2E)
    zkv = project(2048, 2 * LANES)
    store_padded(kb_ref, _rope_block(zkv[:, :LANES], cos, sin_signed, first_half))
    zva = project(1024, 512)
    zvb = zkv[:, LANES:]
    if v_transposed:
        for i in range(zva.shape[1] // LANES):
            va_ref[i * LANES:(i + 1) * LANES, :] = zva[:, i * LANES:(i + 1) * LANES].T.astype(va_ref.dtype)
        store_padded_transposed(vb_ref, zvb)
    else:
        va_ref[...] = zva.astype(va_ref.dtype)
        store_padded(vb_ref, zvb)


def _const_spec(shape):
    zeros = (0,) * len(shape)
    return pl.BlockSpec(shape, lambda *_: zeros, pipeline_mode=pl.Buffered(1))


def _ffn_in(x, cos, sin_signed, n1, wg, wu, wd, n2, win, row_tile, v_transposed):
    rows, d = x.shape
    pos_tiles = cos.shape[0] // row_tile
    row = lambda i: (i, 0)
    out_w = 512
    wide = pl.BlockSpec((row_tile, out_w), row)
    wide_shape = jax.ShapeDtypeStruct((rows, out_w), BF16)
    v_spec = pl.BlockSpec((out_w, row_tile), lambda i: (0, i)) if v_transposed else wide
    v_shape = jax.ShapeDtypeStruct((out_w, rows), BF16) if v_transposed else wide_shape
    return pl.pallas_call(
        functools.partial(_ffn_in_kernel, v_transposed=v_transposed),
        grid=(rows // row_tile,),
        in_specs=[
            pl.BlockSpec((row_tile, d), row),
            pl.BlockSpec((row_tile, LANES), lambda i: (i % pos_tiles, 0)),
            pl.BlockSpec((row_tile, LANES), lambda i: (i % pos_tiles, 0)),
            _const_spec(n1.shape), _const_spec(wg.shape), _const_spec(wu.shape), _const_spec(wd.shape),
            _const_spec(n2.shape), _const_spec(win.shape),
        ],
        out_specs=[pl.BlockSpec((row_tile, d), row), wide, wide, v_spec, wide, wide, v_spec],
        out_shape=[jax.ShapeDtypeStruct((rows, d), F32), wide_shape, wide_shape, v_shape, wide_shape, wide_shape,
                   v_shape],
        compiler_params=pltpu.CompilerParams(dimension_semantics=("parallel",),
                                             vmem_limit_bytes=VMEM_LIMIT_ROWWISE),
        name="ffn_in",
    )(x, cos, sin_signed, n1, wg, wu, wd, n2, win)


def _diff_attn_kernel(lq1_ref, lk1_ref, lq2_ref, lk2_ref, gain_ref, q_ref, km_ref, vmt_ref, k_ref, vt_ref,
                      o_ref, qz_sc, sa_sc, sb_sc, sm_sc, m_sc, acc_sc):
    lam = (jnp.exp(jnp.sum(lq1_ref[...] * lk1_ref[...], keepdims=True))
           - jnp.exp(jnp.sum(lq2_ref[...] * lk2_ref[...], keepdims=True)) + LAMBDA_INIT)
    tq = sa_sc.shape[2]
    tk = sa_sc.shape[1]
    vdim = vt_ref.shape[0]
    nq = q_ref.shape[0] // tq
    nk = vt_ref.shape[1] // tk

    def with_ones(vt):
        return jnp.concatenate([vt, jnp.ones((BF16_SUBLANES, vt.shape[1]), BF16)], axis=0)

    def load_q(i, slot):
        q = q_ref[pl.ds(pl.multiple_of(i * tq, tq), tq), :]
        lane = lax.broadcasted_iota(jnp.int32, q.shape, 1)
        zero = jnp.zeros_like(q)
        qz_sc[slot, 0] = jnp.where(lane < HEAD_DIM, q, zero)
        qz_sc[slot, 1] = jnp.where(lane < HEAD_DIM, zero, q)

    def produce(t, s_sc, slot):
        kt = k_ref[pl.ds(pl.multiple_of(t * tk, tk), tk), :]
        maxima = []
        for c in range(2):
            s = lax.dot_general(kt, qz_sc[slot, c], _NT, preferred_element_type=F32)
            s_sc[c] = s
            maxima.append(jnp.max(s, axis=0, keepdims=True))
        return tuple(maxima)

    def fold(c, s_ref, s_max, vt):
        m_prev = m_sc[c]
        m_new = jnp.maximum(m_prev, s_max)
        alpha = jnp.exp2(m_prev - m_new)
        m_sc[c] = m_new
        for n0 in range(0, tq, MXU_WIDTH):
            cols = slice(n0, n0 + MXU_WIDTH)
            p = jnp.exp2(s_ref[:, cols] - m_new[:, cols])
            acc_sc[c, :, cols] = (alpha[:, cols] * acc_sc[c, :, cols]
                                  + jnp.dot(vt, p.astype(BF16), preferred_element_type=F32))

    def step(t, s_sc, maxima, t_next, s_next_sc, slot_next):
        vt = with_ones(vt_ref[:, pl.ds(pl.multiple_of(t * tk, tk), tk)])
        stats = []
        for c in range(2):
            m_prev = m_sc[c]
            m_new = jnp.maximum(m_prev, maxima[c])
            stats.append((m_new, jnp.exp2(m_prev - m_new)))
            m_sc[c] = m_new
        pieces = [(c, n0) for c in range(2) for n0 in range(0, tq, MXU_WIDTH)]
        sub = tk // len(pieces)
        next_max = [None, None]
        for i, (c, n0) in enumerate(pieces):
            kt = k_ref[pl.ds(pl.multiple_of(t_next * tk + i * sub, sub), sub), :]
            for cc in range(2):
                s = lax.dot_general(kt, qz_sc[slot_next, cc], _NT, preferred_element_type=F32)
                s_next_sc[cc, i * sub:(i + 1) * sub, :] = s
                s_max = jnp.max(s, axis=0, keepdims=True)
                next_max[cc] = s_max if next_max[cc] is None else jnp.maximum(next_max[cc], s_max)
            m_new, alpha = stats[c]
            cols = slice(n0, n0 + MXU_WIDTH)
            p = jnp.exp2(s_sc[c, :, cols] - m_new[:, cols])
            acc_sc[c, :, cols] = (alpha[:, cols] * acc_sc[c, :, cols]
                                  + jnp.dot(vt, p.astype(BF16), preferred_element_type=F32))
        return tuple(next_max)

    def meta_scores(slot):
        for c in range(2):
            sm_sc[c] = lax.dot_general(km_ref[...], qz_sc[slot, c], _NT, preferred_element_type=F32)

    def consume_meta():
        vmt = with_ones(vmt_ref[...])
        for c in range(2):
            fold(c, sm_sc.at[c], jnp.max(sm_sc[c], axis=0, keepdims=True), vmt)

    def finalize(i):
        def normalised(c):
            acc = acc_sc[c]
            return acc[:vdim] * (1.0 / acc[vdim:vdim + 1])

        a = normalised(0) - lam * normalised(1)
        ms = jnp.mean(a * a, axis=0, keepdims=True)
        y = a * lax.rsqrt(ms + EPS) * gain_ref[...] * (1.0 - LAMBDA_INIT)
        o_ref[pl.ds(pl.multiple_of(i * tq, tq), tq), :] = y.T.astype(o_ref.dtype)

    def query_tile(i, maxima_a):
        slot = i % 2
        for c in range(2):
            m_sc[c] = jnp.full(m_sc.shape[1:], NEG, F32)
            acc_sc[c] = jnp.zeros(acc_sc.shape[1:], F32)

        def pair(j, maxima_a):
            t = 2 * j
            maxima_b = step(t, sa_sc, maxima_a, t + 1, sb_sc, slot)
            return step(t + 1, sb_sc, maxima_b, t + 2, sa_sc, slot)

        maxima_a = lax.fori_loop(0, nk // 2 - 1, pair, maxima_a)
        meta_scores(slot)
        maxima_b = step(nk - 2, sa_sc, maxima_a, nk - 1, sb_sc, slot)
        load_q(jnp.minimum(i + 1, nq - 1), 1 - slot)
        consume_meta()
        maxima_next = step(nk - 1, sb_sc, maxima_b, 0, sa_sc, 1 - slot)
        finalize(i)
        return maxima_next

    load_q(0, 0)
    lax.fori_loop(0, nq, query_tile, produce(0, sa_sc, 0))


def _diff_attn(lams, gain_col, qa, ka_meta, vat_meta, ka, vat, batch):
    rows = qa.shape[0]
    seq = rows // batch
    vdim = 2 * HEAD_DIM
    tq, tk = DIFF_TQ, DIFF_TK
    assert (seq // tk) % 2 == 0
    lam_spec = pl.BlockSpec((1, HEAD_DIM), lambda b, h: (0, 0))
    return pl.pallas_call(
        _diff_attn_kernel,
        grid=(batch, DIFF_HEADS),
        in_specs=[
            lam_spec, lam_spec, lam_spec, lam_spec,
            pl.BlockSpec((vdim, 1), lambda b, h: (0, 0)),
            pl.BlockSpec((seq, LANES), lambda b, h: (b, h)),
            pl.BlockSpec((N_META, LANES), lambda b, h: (0, h)),
            pl.BlockSpec((vdim, N_META), lambda b, h: (h, 0)),
            pl.BlockSpec((seq, LANES), lambda b, h: (b, h)),
            pl.BlockSpec((vdim, seq), lambda b, h: (h, b)),
        ],
        out_specs=pl.BlockSpec((seq, LANES), lambda b, h: (b, h)),
        out_shape=jax.ShapeDtypeStruct((rows, DIFF_HEADS * vdim), BF16),
        scratch_shapes=[pltpu.VMEM((2, 2, tq, LANES), BF16),
                        pltpu.VMEM((2, tk, tq), F32), pltpu.VMEM((2, tk, tq), F32),
                        pltpu.VMEM((2, N_META, tq), F32),
                        pltpu.VMEM((2, 1, tq), F32),
                        pltpu.VMEM((2, vdim + BF16_SUBLANES, tq), F32)],
        compiler_params=pltpu.CompilerParams(dimension_semantics=("parallel", "parallel"),
                                             vmem_limit_bytes=VMEM_LIMIT_ATTN),
        name="diff_attn",
    )(*lams, gain_col, qa, ka_meta, vat_meta, ka, vat)


def _win_attn_kernel(sink_ref, q_ref, km_ref, vmt_ref, k_ref, vt_ref, o_ref,
                     delta_sc, sa_sc, sb_sc, sma_sc, smb_sc):
    g = pl.program_id(1)
    blk = pl.program_id(2)
    seq = k_ref.shape[0]
    tq, span = WIN_TQ, WIN_SPAN
    n = 2 * tq
    nt = q_ref.shape[0] // tq
    delta_sc[...] = (lax.broadcasted_iota(jnp.int32, (span, tq), 0)
                     - lax.broadcasted_iota(jnp.int32, (span, tq), 1))
    first_pair = lax.broadcasted_iota(jnp.int32, (1, n), 1) < tq
    sinks = [jnp.where(first_pair, sink_ref[g * WIN_GROUP + par], sink_ref[g * WIN_GROUP + 2 + par]) * LOG2E
             for par in range(2)]

    def key_start(i):
        t0 = blk * (nt * tq) + i * tq
        return t0, pl.multiple_of(jnp.clip(t0 - WINDOW, 0, seq - span), WINDOW)

    def produce(i, s_sc, sm_sc):
        t0, ks = key_start(i)
        off = ks - t0
        delta = delta_sc[...]
        valid = (delta >= -WINDOW - off) & (delta <= WINDOW - off)
        rows = pl.ds(pl.multiple_of(i * tq, tq), tq)
        q_cat = jnp.concatenate([q_ref[rows, :LANES], q_ref[rows, LANES:]], axis=0)
        maxima = []
        for par in range(2):
            part = slice(par * LANES, (par + 1) * LANES)
            s = lax.dot_general(k_ref[pl.ds(ks, span), part], q_cat, _NT, preferred_element_type=F32)
            s = jnp.concatenate([jnp.where(valid, s[:, :tq], NEG), jnp.where(valid, s[:, tq:], NEG)], axis=1)
            sm = lax.dot_general(km_ref[:, part], q_cat, _NT, preferred_element_type=F32)
            s_sc[par] = s
            sm_sc[par] = sm
            maxima.append(jnp.maximum(jnp.maximum(jnp.max(s, axis=0, keepdims=True),
                                                  jnp.max(sm, axis=0, keepdims=True)), sinks[par]))
        return tuple(maxima)

    def consume(i, s_sc, sm_sc, maxima):
        _, ks = key_start(i)
        o_t = None
        for par in range(2):
            part = slice(par * LANES, (par + 1) * LANES)
            m = maxima[par]
            p = jnp.exp2(s_sc[par] - m)
            pm = jnp.exp2(sm_sc[par] - m)
            l = jnp.sum(p, axis=0, keepdims=True) + jnp.sum(pm, axis=0, keepdims=True) + jnp.exp2(sinks[par] - m)
            o = (jnp.dot(vt_ref[part, pl.ds(ks, span)], p.astype(BF16), preferred_element_type=F32)
                 + jnp.dot(vmt_ref[part, :], pm.astype(BF16), preferred_element_type=F32)) * (1.0 / l)
            o_t = o if o_t is None else o_t + o
        rows = pl.ds(pl.multiple_of(i * tq, tq), tq)
        for pair in range(WIN_GROUP // 2):
            o_ref[rows, pair * LANES:(pair + 1) * LANES] = o_t[:, pair * tq:(pair + 1) * tq].T.astype(o_ref.dtype)

    def pair_of_tiles(j, maxima_a):
        i = 2 * j
        maxima_b = produce(i + 1, sb_sc, smb_sc)
        consume(i, sa_sc, sma_sc, maxima_a)
        maxima_a = produce(i + 2, sa_sc, sma_sc)
        consume(i + 1, sb_sc, smb_sc, maxima_b)
        return maxima_a

    maxima_a = lax.fori_loop(0, nt // 2 - 1, pair_of_tiles, produce(0, sa_sc, sma_sc), unroll=5)
    maxima_b = produce(nt - 1, sb_sc, smb_sc)
    consume(nt - 2, sa_sc, sma_sc, maxima_a)
    consume(nt - 1, sb_sc, smb_sc, maxima_b)


def _win_attn(sink, qb, kb_meta, vbt_meta, kb, vbt, batch):
    rows = qb.shape[0]
    seq = rows // batch
    nblk = seq // WIN_QBLOCK
    assert (WIN_QBLOCK // WIN_TQ) % 2 == 0
    width = 2 * LANES
    n = 2 * WIN_TQ
    return pl.pallas_call(
        _win_attn_kernel,
        grid=(batch, WIN_KV_HEADS, nblk),
        in_specs=[
            pl.BlockSpec(memory_space=pltpu.SMEM),
            pl.BlockSpec((WIN_QBLOCK, width), lambda b, g, i: (b * nblk + i, g)),
            pl.BlockSpec((N_META, width), lambda b, g, i: (0, g)),
            pl.BlockSpec((width, N_META), lambda b, g, i: (g, 0)),
            pl.BlockSpec((seq, width), lambda b, g, i: (b, g)),
            pl.BlockSpec((width, seq), lambda b, g, i: (g, b)),
        ],
        out_specs=pl.BlockSpec((WIN_QBLOCK, width), lambda b, g, i: (b * nblk + i, g)),
        out_shape=jax.ShapeDtypeStruct((rows, WIN_KV_HEADS * width), BF16),
        scratch_shapes=[pltpu.VMEM((WIN_SPAN, WIN_TQ), jnp.int32),
                        pltpu.VMEM((2, WIN_SPAN, n), F32), pltpu.VMEM((2, WIN_SPAN, n), F32),
                        pltpu.VMEM((2, N_META, n), F32), pltpu.VMEM((2, N_META, n), F32)],
        compiler_params=pltpu.CompilerParams(dimension_semantics=("parallel", "parallel", "arbitrary"),
                                             vmem_limit_bytes=VMEM_LIMIT_ATTN),
        name="win_attn",
    )(sink, qb, kb_meta, vbt_meta, kb, vbt)


def _out_ffn_kernel(h_ref, oa_ref, ob_ref, wn_ref, wo_ref, n_ref, wg_ref, wu_ref, wd_ref, fn_ref, out_ref):
    half = oa_ref.shape[1]
    ob, ob_scale = _rms_split(ob_ref[...].astype(F32), wn_ref[...])
    mix = (jnp.dot(oa_ref[...], wo_ref[:half, :], preferred_element_type=F32)
           + jnp.dot(ob, wo_ref[half:, :], preferred_element_type=F32) * ob_scale)
    h = h_ref[...] + mix
    hg, h_scale = _rms_split(h, n_ref[...])
    h = h + 0.5 * _swiglu(hg, wg_ref, wu_ref, wd_ref, h_scale)
    out_ref[...] = _rms(h, fn_ref[...])


def _out_ffn(h1, oa, ob, wn, wo, n, wg, wu, wd, fn):
    rows, d = h1.shape
    half = oa.shape[1]
    row = lambda i: (i, 0)
    return pl.pallas_call(
        _out_ffn_kernel,
        grid=(rows // ROW_TILE,),
        in_specs=[
            pl.BlockSpec((ROW_TILE, d), row), pl.BlockSpec((ROW_TILE, half), row), pl.BlockSpec((ROW_TILE, half), row),
            _const_spec(wn.shape), _const_spec(wo.shape), _const_spec(n.shape),
            _const_spec(wg.shape), _const_spec(wu.shape), _const_spec(wd.shape), _const_spec(fn.shape),
        ],
        out_specs=pl.BlockSpec((ROW_TILE, d), row),
        out_shape=jax.ShapeDtypeStruct((rows, d), F32),
        compiler_params=pltpu.CompilerParams(dimension_semantics=("parallel",),
                                             vmem_limit_bytes=VMEM_LIMIT_ROWWISE),
        name="out_ffn",
    )(h1, oa, ob, wn, wo, n, wg, wu, wd, fn)


def _rope_tables(length):
    pos = jnp.arange(length, dtype=F32)
    inv = ROPE_THETA ** (-jnp.arange(0, HEAD_DIM, 2, dtype=F32) / HEAD_DIM)
    ang = pos[:, None] * inv[None, :]
    cos, sin = jnp.cos(ang), jnp.sin(ang)
    return jnp.tile(cos, (1, 4)), jnp.tile(jnp.concatenate([-sin, sin], axis=1), (1, 2))


def kernel(x, meta_tokens, ffn1_norm, ffn1_w_gate, ffn1_w_up, ffn1_w_down, mix_norm, w_in, lambda_q1, lambda_k1, lambda_q2, lambda_k2, diff_norm, win_sink, win_norm, w_out, ffn2_norm, ffn2_w_gate, ffn2_w_up, ffn2_w_down, final_norm):
    batch, seq, d = x.shape
    assert ffn1_norm.shape[0] == 1, "single layer only"
    assert seq % ROW_TILE == 0 and seq % DIFF_TQ == 0 and seq % DIFF_TK == 0 and seq % WIN_QBLOCK == 0

    cos, sin_signed = _rope_tables(N_META + seq)
    ffn_in_weights = (ffn1_norm, ffn1_w_gate[0].astype(BF16), ffn1_w_up[0].astype(BF16),
                      ffn1_w_down[0].astype(BF16), mix_norm, w_in[0].astype(BF16))

    meta = _ffn_in(meta_tokens.astype(x.dtype), cos[:N_META], sin_signed[:N_META], *ffn_in_weights, N_META,
                   v_transposed=False)
    real = _ffn_in(x.reshape(batch * seq, d), cos[N_META:], sin_signed[N_META:], *ffn_in_weights, ROW_TILE,
                   v_transposed=True)
    _, _, ka_m, va_m, _, kb_m, vb_m = meta
    h1, qa, ka, vat, qb, kb, vbt = real

    lams = (lambda_q1, lambda_k1, lambda_q2, lambda_k2)
    out_a = _diff_attn(lams, diff_norm.reshape(2 * HEAD_DIM, 1), qa, ka_m, va_m.T, ka, vat, batch)

    out_b = _win_attn(win_sink.reshape(-1), qb, kb_m, vb_m.T, kb, vbt, batch)

    out = _out_ffn(h1, out_a, out_b, win_norm, w_out[0].astype(BF16), ffn2_norm,
                   ffn2_w_gate[0].astype(BF16), ffn2_w_up[0].astype(BF16), ffn2_w_down[0].astype(BF16),
                   final_norm.reshape(1, d))
    return out.reshape(batch, seq, d)
```

```python
import functools
import math

import jax
import jax.numpy as jnp
from jax import lax
from jax.experimental import pallas as pl
from jax.experimental.pallas import tpu as pltpu

N_META = 16
HEAD_DIM = 64
DIFF_HEADS = 4
WIN_KV_HEADS = 2
WIN_GROUP = 4
WINDOW = 128
ROPE_THETA = 10000.0
EPS = 1e-6
NEG = -1e30
LAMBDA_INIT = 0.8 - 0.6 * math.exp(-0.3 * 0)
QK_SCALE = HEAD_DIM ** -0.5
LOG2E = math.log2(math.e)

LANES = 128
BF16_SUBLANES = 16
VMEM_LIMIT_ROWWISE = 56 * 1024 * 1024
VMEM_LIMIT_ATTN = 40 * 1024 * 1024

ROW_TILE = 512
MXU_WIDTH = 256
FF_CHUNK = 6 * MXU_WIDTH
DIFF_TQ = 512
DIFF_TK = 1024
WIN_TQ = 128
WIN_SPAN = WIN_TQ + 2 * WINDOW
WIN_QBLOCK = 4096

F32 = jnp.float32
BF16 = jnp.bfloat16
_NT = (((1,), (1,)), ((), ()))


def _rms(x, gain):
    return x * lax.rsqrt(jnp.mean(x * x, axis=-1, keepdims=True) + EPS) * gain


def _rms_split(x, gain):
    return (x * gain).astype(BF16), lax.rsqrt(jnp.mean(x * x, axis=-1, keepdims=True) + EPS)


def _swiglu(xn, wg_ref, wu_ref, wd_ref, row_scale=None):
    d_ff = wg_ref.shape[1]
    acc = None
    for start in range(0, d_ff, FF_CHUNK):
        sl = slice(start, min(start + FF_CHUNK, d_ff))
        g = jnp.dot(xn, wg_ref[:, sl], preferred_element_type=F32)
        u = jnp.dot(xn, wu_ref[:, sl], preferred_element_type=F32)
        if row_scale is not None:
            g = g * row_scale
            u = u * row_scale
        a = (g * jax.nn.sigmoid(g) * u).astype(BF16)
        d = jnp.dot(a, wd_ref[sl, :], preferred_element_type=F32)
        acc = d if acc is None else acc + d
    return acc


def _rope_block(x, cos, sin_signed, first_half):
    partner = jnp.where(first_half, pltpu.roll(x, LANES - 32, 1), pltpu.roll(x, 32, 1))
    return x * cos + partner * sin_signed


def _ffn_in_kernel(x_ref, cos_ref, sin_ref, n1_ref, wg_ref, wu_ref, wd_ref, n2_ref, win_ref,
                   h_ref, qa_ref, ka_ref, va_ref, qb_ref, kb_ref, vb_ref, *, v_transposed):
    x = x_ref[...]
    xg, x_scale = _rms_split(x, n1_ref[...])
    h = x + 0.5 * _swiglu(xg, wg_ref, wu_ref, wd_ref, x_scale)
    h_ref[...] = h
    u, h_scale = _rms_split(h, n2_ref[...])

    rows = x.shape[0]
    cos = cos_ref[...]
    sin_signed = sin_ref[...]
    lane = lax.broadcasted_iota(jnp.int32, (rows, LANES), 1)
    first_half = (lane % HEAD_DIM) < (HEAD_DIM // 2)
    low = lane < HEAD_DIM

    def project(col0, width):
        return jnp.dot(u, win_ref[:, col0:col0 + width], preferred_element_type=F32) * h_scale

    def store_rope(dst_ref, z, scale):
        for i in range(z.shape[1] // LANES):
            blk = _rope_block(z[:, i * LANES:(i + 1) * LANES], cos, sin_signed, first_half)
            if scale != 1.0:
                blk = blk * scale
            dst_ref[:, i * LANES:(i + 1) * LANES] = blk.astype(dst_ref.dtype)

    def store_padded(dst_ref, z):
        swapped = pltpu.roll(z, HEAD_DIM, 1)
        zero = jnp.zeros_like(z)
        parts = (jnp.where(low, z, zero), jnp.where(low, zero, swapped),
                 jnp.where(low, swapped, zero), jnp.where(low, zero, z))
        for i, part in enumerate(parts):
            dst_ref[:, i * LANES:(i + 1) * LANES] = part.astype(dst_ref.dtype)

    def store_padded_transposed(dst_ref, z):
        zt = z.T
        swapped = jnp.concatenate([zt[HEAD_DIM:], zt[:HEAD_DIM]], axis=0)
        top = lax.broadcasted_iota(jnp.int32, zt.shape, 0) < HEAD_DIM
        zero = jnp.zeros_like(zt)
        parts = (jnp.where(top, zt, zero), jnp.where(top, zero, swapped),
                 jnp.where(top, swapped, zero), jnp.where(top, zero, zt))
        for i, part in enumerate(parts):
            dst_ref[i * LANES:(i + 1) * LANES, :] = part.astype(dst_ref.dtype)

    store_rope(qa_ref, project(0, 512), QK_SCALE * LOG2E)
    store_rope(ka_ref, project(512, 512), 1.0)
    store_rope(qb_ref, project(1536, 512), QK_SCALE * LOG2E)
    zkv = project(2048, 2 * LANES)
    store_padded(kb_ref, _rope_block(zkv[:, :LANES], cos, sin_signed, first_half))
    zva = project(1024, 512)
    zvb = zkv[:, LANES:]
    if v_transposed:
        for i in range(zva.shape[1] // LANES):
            va_ref[i * LANES:(i + 1) * LANES, :] = zva[:, i * LANES:(i + 1) * LANES].T.astype(va_ref.dtype)
        store_padded_transposed(vb_ref, zvb)
    else:
        va_ref[...] = zva.astype(va_ref.dtype)
        store_padded(vb_ref, zvb)


def _const_spec(shape):
    zeros = (0,) * len(shape)
    return pl.BlockSpec(shape, lambda *_: zeros, pipeline_mode=pl.Buffered(1))


def _ffn_in(x, cos, sin_signed, n1, wg, wu, wd, n2, win, row_tile, v_transposed):
    rows, d = x.shape
    pos_tiles = cos.shape[0] // row_tile
    row = lambda i: (i, 0)
    out_w = 512
    wide = pl.BlockSpec((row_tile, out_w), row)
    wide_shape = jax.ShapeDtypeStruct((rows, out_w), BF16)
    v_spec = pl.BlockSpec((out_w, row_tile), lambda i: (0, i)) if v_transposed else wide
    v_shape = jax.ShapeDtypeStruct((out_w, rows), BF16) if v_transposed else wide_shape
    return pl.pallas_call(
        functools.partial(_ffn_in_kernel, v_transposed=v_transposed),
        grid=(rows // row_tile,),
        in_specs=[
            pl.BlockSpec((row_tile, d), row),
            pl.BlockSpec((row_tile, LANES), lambda i: (i % pos_tiles, 0)),
            pl.BlockSpec((row_tile, LANES), lambda i: (i % pos_tiles, 0)),
            _const_spec(n1.shape), _const_spec(wg.shape), _const_spec(wu.shape), _const_spec(wd.shape),
            _const_spec(n2.shape), _const_spec(win.shape),
        ],
        out_specs=[pl.BlockSpec((row_tile, d), row), wide, wide, v_spec, wide, wide, v_spec],
        out_shape=[jax.ShapeDtypeStruct((rows, d), F32), wide_shape, wide_shape, v_shape, wide_shape, wide_shape,
                   v_shape],
        compiler_params=pltpu.CompilerParams(dimension_semantics=("parallel",),
                                             vmem_limit_bytes=VMEM_LIMIT_ROWWISE),
        name="ffn_in",
    )(x, cos, sin_signed, n1, wg, wu, wd, n2, win)


def _diff_attn_kernel(lq1_ref, lk1_ref, lq2_ref, lk2_ref, gain_ref, q_ref, km_ref, vmt_ref, k_ref, vt_ref,
                      o_ref, qz_sc, sa_sc, sb_sc, sm_sc, m_sc, acc_sc):
    lam = (jnp.exp(jnp.sum(lq1_ref[...] * lk1_ref[...], keepdims=True))
           - jnp.exp(jnp.sum(lq2_ref[...] * lk2_ref[...], keepdims=True)) + LAMBDA_INIT)
    tq = sa_sc.shape[2]
    tk = sa_sc.shape[1]
    vdim = vt_ref.shape[0]
    nq = q_ref.shape[0] // tq
    nk = vt_ref.shape[1] // tk

    def with_ones(vt):
        return jnp.concatenate([vt, jnp.ones((BF16_SUBLANES, vt.shape[1]), BF16)], axis=0)

    def load_q(i, slot):
        q = q_ref[pl.ds(pl.multiple_of(i * tq, tq), tq), :]
        lane = lax.broadcasted_iota(jnp.int32, q.shape, 1)
        zero = jnp.zeros_like(q)
        qz_sc[slot, 0] = jnp.where(lane < HEAD_DIM, q, zero)
        qz_sc[slot, 1] = jnp.where(lane < HEAD_DIM, zero, q)

    def produce(t, s_sc, slot):
        kt = k_ref[pl.ds(pl.multiple_of(t * tk, tk), tk), :]
        maxima = []
        for c in range(2):
            s = lax.dot_general(kt, qz_sc[slot, c], _NT, preferred_element_type=F32)
            s_sc[c] = s
            maxima.append(jnp.max(s, axis=0, keepdims=True))
        return tuple(maxima)

    def fold(c, s_ref, s_max, vt):
        m_prev = m_sc[c]
        m_new = jnp.maximum(m_prev, s_max)
        alpha = jnp.exp2(m_prev - m_new)
        m_sc[c] = m_new
        for n0 in range(0, tq, MXU_WIDTH):
            cols = slice(n0, n0 + MXU_WIDTH)
            p = jnp.exp2(s_ref[:, cols] - m_new[:, cols])
            acc_sc[c, :, cols] = (alpha[:, cols] * acc_sc[c, :, cols]
                                  + jnp.dot(vt, p.astype(BF16), preferred_element_type=F32))

    def step(t, s_sc, maxima, t_next, s_next_sc, slot_next):
        vt = with_ones(vt_ref[:, pl.ds(pl.multiple_of(t * tk, tk), tk)])
        stats = []
        for c in range(2):
            m_prev = m_sc[c]
            m_new = jnp.maximum(m_prev, maxima[c])
            stats.append((m_new, jnp.exp2(m_prev - m_new)))
            m_sc[c] = m_new
        pieces = [(c, n0) for c in range(2) for n0 in range(0, tq, MXU_WIDTH)]
        sub = tk // len(pieces)
        next_max = [None, None]
        for i, (c, n0) in enumerate(pieces):
            kt = k_ref[pl.ds(pl.multiple_of(t_next * tk + i * sub, sub), sub), :]
            for cc in range(2):
                s = lax.dot_general(kt, qz_sc[slot_next, cc], _NT, preferred_element_type=F32)
                s_next_sc[cc, i * sub:(i + 1) * sub, :] = s
                s_max = jnp.max(s, axis=0, keepdims=True)
                next_max[cc] = s_max if next_max[cc] is None else jnp.maximum(next_max[cc], s_max)
            m_new, alpha = stats[c]
            cols = slice(n0, n0 + MXU_WIDTH)
            p = jnp.exp2(s_sc[c, :, cols] - m_new[:, cols])
            acc_sc[c, :, cols] = (alpha[:, cols] * acc_sc[c, :, cols]
                                  + jnp.dot(vt, p.astype(BF16), preferred_element_type=F32))
        return tuple(next_max)

    def meta_scores(slot):
        for c in range(2):
            sm_sc[c] = lax.dot_general(km_ref[...], qz_sc[slot, c], _NT, preferred_element_type=F32)

    def consume_meta():
        vmt = with_ones(vmt_ref[...])
        for c in range(2):
            fold(c, sm_sc.at[c], jnp.max(sm_sc[c], axis=0, keepdims=True), vmt)

    def finalize(i):
        def normalised(c):
            acc = acc_sc[c]
            return acc[:vdim] * (1.0 / acc[vdim:vdim + 1])

        a = normalised(0) - lam * normalised(1)
        ms = jnp.mean(a * a, axis=0, keepdims=True)
        y = a * lax.rsqrt(ms + EPS) * gain_ref[...] * (1.0 - LAMBDA_INIT)
        o_ref[pl.ds(pl.multiple_of(i * tq, tq), tq), :] = y.T.astype(o_ref.dtype)

    def query_tile(i, maxima_a):
        slot = i % 2
        for c in range(2):
            m_sc[c] = jnp.full(m_sc.shape[1:], NEG, F32)
            acc_sc[c] = jnp.zeros(acc_sc.shape[1:], F32)

        def pair(j, maxima_a):
            t = 2 * j
            maxima_b = step(t, sa_sc, maxima_a, t + 1, sb_sc, slot)
            return step(t + 1, sb_sc, maxima_b, t + 2, sa_sc, slot)

        maxima_a = lax.fori_loop(0, nk // 2 - 1, pair, maxima_a)
        meta_scores(slot)
        maxima_b = step(nk - 2, sa_sc, maxima_a, nk - 1, sb_sc, slot)
        load_q(jnp.minimum(i + 1, nq - 1), 1 - slot)
        consume_meta()
        maxima_next = step(nk - 1, sb_sc, maxima_b, 0, sa_sc, 1 - slot)
        finalize(i)
        return maxima_next

    load_q(0, 0)
    lax.fori_loop(0, nq, query_tile, produce(0, sa_sc, 0))


def _diff_attn(lams, gain_col, qa, ka_meta, vat_meta, ka, vat, batch):
    rows = qa.shape[0]
    seq = rows // batch
    vdim = 2 * HEAD_DIM
    tq, tk = DIFF_TQ, DIFF_TK
    assert (seq // tk) % 2 == 0
    lam_spec = pl.BlockSpec((1, HEAD_DIM), lambda b, h: (0, 0))
    return pl.pallas_call(
        _diff_attn_kernel,
        grid=(batch, DIFF_HEADS),
        in_specs=[
            lam_spec, lam_spec, lam_spec, lam_spec,
            pl.BlockSpec((vdim, 1), lambda b, h: (0, 0)),
            pl.BlockSpec((seq, LANES), lambda b, h: (b, h)),
            pl.BlockSpec((N_META, LANES), lambda b, h: (0, h)),
            pl.BlockSpec((vdim, N_META), lambda b, h: (h, 0)),
            pl.BlockSpec((seq, LANES), lambda b, h: (b, h)),
            pl.BlockSpec((vdim, seq), lambda b, h: (h, b)),
        ],
        out_specs=pl.BlockSpec((seq, LANES), lambda b, h: (b, h)),
        out_shape=jax.ShapeDtypeStruct((rows, DIFF_HEADS * vdim), BF16),
        scratch_shapes=[pltpu.VMEM((2, 2, tq, LANES), BF16),
                        pltpu.VMEM((2, tk, tq), F32), pltpu.VMEM((2, tk, tq), F32),
                        pltpu.VMEM((2, N_META, tq), F32),
                        pltpu.VMEM((2, 1, tq), F32),
                        pltpu.VMEM((2, vdim + BF16_SUBLANES, tq), F32)],
        compiler_params=pltpu.CompilerParams(dimension_semantics=("parallel", "parallel"),
                                             vmem_limit_bytes=VMEM_LIMIT_ATTN),
        name="diff_attn",
    )(*lams, gain_col, qa, ka_meta, vat_meta, ka, vat)


def _win_attn_kernel(sink_ref, q_ref, km_ref, vmt_ref, k_ref, vt_ref, o_ref,
                     delta_sc, sa_sc, sb_sc, sma_sc, smb_sc):
    g = pl.program_id(1)
    blk = pl.program_id(2)
    seq = k_ref.shape[0]
    tq, span = WIN_TQ, WIN_SPAN
    n = 2 * tq
    nt = q_ref.shape[0] // tq
    delta_sc[...] = (lax.broadcasted_iota(jnp.int32, (span, tq), 0)
                     - lax.broadcasted_iota(jnp.int32, (span, tq), 1))
    first_pair = lax.broadcasted_iota(jnp.int32, (1, n), 1) < tq
    sinks = [jnp.where(first_pair, sink_ref[g * WIN_GROUP + par], sink_ref[g * WIN_GROUP + 2 + par]) * LOG2E
             for par in range(2)]

    ones_row = (HEAD_DIM, 0)
    top_half = lax.broadcasted_iota(jnp.int32, (LANES, n), 0) < HEAD_DIM

    def key_start(i):
        t0 = blk * (nt * tq) + i * tq
        return t0, pl.multiple_of(jnp.clip(t0 - WINDOW, 0, seq - span), WINDOW)

    def produce_pieces(i, s_sc, sm_sc):
        t0, ks = key_start(i)
        off = ks - t0
        rows = pl.ds(pl.multiple_of(i * tq, tq), tq)

        def piece(par):
            delta = delta_sc[...]
            valid = (delta >= -WINDOW - off) & (delta <= WINDOW - off)
            q_cat = jnp.concatenate([q_ref[rows, :LANES], q_ref[rows, LANES:]], axis=0)
            part = slice(par * LANES, (par + 1) * LANES)
            k_cat = jnp.concatenate([k_ref[pl.ds(ks, span), part], km_ref[:, part]], axis=0)
            s_all = lax.dot_general(k_cat, q_cat, _NT, preferred_element_type=F32)
            s = s_all[:span]
            s = jnp.concatenate([jnp.where(valid, s[:, :tq], NEG), jnp.where(valid, s[:, tq:], NEG)], axis=1)
            sm = s_all[span:]
            s_sc[par] = s
            sm_sc[par] = sm
            return jnp.maximum(jnp.maximum(jnp.max(s, axis=0, keepdims=True),
                                           jnp.max(sm, axis=0, keepdims=True)), sinks[par])

        return [functools.partial(piece, par) for par in range(2)]

    def consume_pieces(i, s_sc, sm_sc, maxima):
        _, ks = key_start(i)

        def with_ones(vt, par):
            r = ones_row[par]
            ones = jnp.ones((BF16_SUBLANES, vt.shape[1]), BF16)
            return jnp.concatenate(([vt[:r]] if r else []) + [ones, vt[r + BF16_SUBLANES:]], axis=0)

        def piece(par):
            part = slice(par * LANES, (par + 1) * LANES)
            m = maxima[par]
            p = jnp.exp2(s_sc[par] - m)
            pm = jnp.exp2(sm_sc[par] - m)
            o = (jnp.dot(with_ones(vt_ref[part, pl.ds(ks, span)], par), p.astype(BF16),
                         preferred_element_type=F32)
                 + jnp.dot(with_ones(vmt_ref[part, :], par), pm.astype(BF16), preferred_element_type=F32))
            l = o[ones_row[par]:ones_row[par] + 1] + jnp.exp2(sinks[par] - m)
            return o * (1.0 / l)

        def store(outs):
            o_t = jnp.where(top_half, outs[0], outs[1])
            rows = pl.ds(pl.multiple_of(i * tq, tq), tq)
            for pair in range(WIN_GROUP // 2):
                o_ref[rows, pair * LANES:(pair + 1) * LANES] = (
                    o_t[:, pair * tq:(pair + 1) * tq].T.astype(o_ref.dtype))

        return [functools.partial(piece, par) for par in range(2)], store

    def produce(i, s_sc, sm_sc):
        return tuple(piece() for piece in produce_pieces(i, s_sc, sm_sc))

    def consume(i, s_sc, sm_sc, maxima):
        pieces, store = consume_pieces(i, s_sc, sm_sc, maxima)
        store([piece() for piece in pieces])

    def step(i, s_sc, sm_sc, maxima, i_next, s_next_sc, sm_next_sc):
        makers = produce_pieces(i_next, s_next_sc, sm_next_sc)
        users, store = consume_pieces(i, s_sc, sm_sc, maxima)
        next_maxima, outs = [], []
        for make, use in zip(makers, users):
            next_maxima.append(make())
            outs.append(use())
        store(outs)
        return tuple(next_maxima)

    def pair_of_tiles(j, maxima_a):
        i = 2 * j
        maxima_b = step(i, sa_sc, sma_sc, maxima_a, i + 1, sb_sc, smb_sc)
        return step(i + 1, sb_sc, smb_sc, maxima_b, i + 2, sa_sc, sma_sc)

    maxima_a = lax.fori_loop(0, nt // 2 - 1, pair_of_tiles, produce(0, sa_sc, sma_sc), unroll=5)
    maxima_b = step(nt - 2, sa_sc, sma_sc, maxima_a, nt - 1, sb_sc, smb_sc)
    consume(nt - 1, sb_sc, smb_sc, maxima_b)


def _win_attn(sink, qb, kb_meta, vbt_meta, kb, vbt, batch):
    rows = qb.shape[0]
    seq = rows // batch
    nblk = seq // WIN_QBLOCK
    assert (WIN_QBLOCK // WIN_TQ) % 2 == 0
    width = 2 * LANES
    n = 2 * WIN_TQ
    return pl.pallas_call(
        _win_attn_kernel,
        grid=(batch, WIN_KV_HEADS, nblk),
        in_specs=[
            pl.BlockSpec(memory_space=pltpu.SMEM),
            pl.BlockSpec((WIN_QBLOCK, width), lambda b, g, i: (b * nblk + i, g)),
            pl.BlockSpec((N_META, width), lambda b, g, i: (0, g)),
            pl.BlockSpec((width, N_META), lambda b, g, i: (g, 0)),
            pl.BlockSpec((seq, width), lambda b, g, i: (b, g)),
            pl.BlockSpec((width, seq), lambda b, g, i: (g, b)),
        ],
        out_specs=pl.BlockSpec((WIN_QBLOCK, width), lambda b, g, i: (b * nblk + i, g)),
        out_shape=jax.ShapeDtypeStruct((rows, WIN_KV_HEADS * width), BF16),
        scratch_shapes=[pltpu.VMEM((WIN_SPAN, WIN_TQ), jnp.int32),
                        pltpu.VMEM((2, WIN_SPAN, n), F32), pltpu.VMEM((2, WIN_SPAN, n), F32),
                        pltpu.VMEM((2, N_META, n), F32), pltpu.VMEM((2, N_META, n), F32)],
        compiler_params=pltpu.CompilerParams(dimension_semantics=("parallel", "parallel", "arbitrary"),
                                             vmem_limit_bytes=VMEM_LIMIT_ATTN),
        name="win_attn",
    )(sink, qb, kb_meta, vbt_meta, kb, vbt)


def _out_ffn_kernel(h_ref, oa_ref, ob_ref, wn_ref, wo_ref, n_ref, wg_ref, wu_ref, wd_ref, fn_ref, out_ref):
    half = oa_ref.shape[1]
    ob, ob_scale = _rms_split(ob_ref[...].astype(F32), wn_ref[...])
    mix = (jnp.dot(oa_ref[...], wo_ref[:half, :], preferred_element_type=F32)
           + jnp.dot(ob, wo_ref[half:, :], preferred_element_type=F32) * ob_scale)
    h = h_ref[...] + mix
    hg, h_scale = _rms_split(h, n_ref[...])
    h = h + 0.5 * _swiglu(hg, wg_ref, wu_ref, wd_ref, h_scale)
    out_ref[...] = _rms(h, fn_ref[...])


def _out_ffn(h1, oa, ob, wn, wo, n, wg, wu, wd, fn):
    rows, d = h1.shape
    half = oa.shape[1]
    row = lambda i: (i, 0)
    return pl.pallas_call(
        _out_ffn_kernel,
        grid=(rows // ROW_TILE,),
        in_specs=[
            pl.BlockSpec((ROW_TILE, d), row), pl.BlockSpec((ROW_TILE, half), row), pl.BlockSpec((ROW_TILE, half), row),
            _const_spec(wn.shape), _const_spec(wo.shape), _const_spec(n.shape),
            _const_spec(wg.shape), _const_spec(wu.shape), _const_spec(wd.shape), _const_spec(fn.shape),
        ],
        out_specs=pl.BlockSpec((ROW_TILE, d), row),
        out_shape=jax.ShapeDtypeStruct((rows, d), F32),
        compiler_params=pltpu.CompilerParams(dimension_semantics=("parallel",),
                                             vmem_limit_bytes=VMEM_LIMIT_ROWWISE),
        name="out_ffn",
    )(h1, oa, ob, wn, wo, n, wg, wu, wd, fn)


def _rope_tables(length):
    pos = jnp.arange(length, dtype=F32)
    inv = ROPE_THETA ** (-jnp.arange(0, HEAD_DIM, 2, dtype=F32) / HEAD_DIM)
    ang = pos[:, None] * inv[None, :]
    cos, sin = jnp.cos(ang), jnp.sin(ang)
    return jnp.tile(cos, (1, 4)), jnp.tile(jnp.concatenate([-sin, sin], axis=1), (1, 2))


def kernel(x, meta_tokens, ffn1_norm, ffn1_w_gate, ffn1_w_up, ffn1_w_down, mix_norm, w_in, lambda_q1, lambda_k1, lambda_q2, lambda_k2, diff_norm, win_sink, win_norm, w_out, ffn2_norm, ffn2_w_gate, ffn2_w_up, ffn2_w_down, final_norm):
    batch, seq, d = x.shape
    assert ffn1_norm.shape[0] == 1, "single layer only"
    assert seq % ROW_TILE == 0 and seq % DIFF_TQ == 0 and seq % DIFF_TK == 0 and seq % WIN_QBLOCK == 0

    cos, sin_signed = _rope_tables(N_META + seq)
    ffn_in_weights = (ffn1_norm, ffn1_w_gate[0].astype(BF16), ffn1_w_up[0].astype(BF16),
                      ffn1_w_down[0].astype(BF16), mix_norm, w_in[0].astype(BF16))

    meta = _ffn_in(meta_tokens.astype(x.dtype), cos[:N_META], sin_signed[:N_META], *ffn_in_weights, N_META,
                   v_transposed=False)
    real = _ffn_in(x.reshape(batch * seq, d), cos[N_META:], sin_signed[N_META:], *ffn_in_weights, ROW_TILE,
                   v_transposed=True)
    _, _, ka_m, va_m, _, kb_m, vb_m = meta
    h1, qa, ka, vat, qb, kb, vbt = real

    lams = (lambda_q1, lambda_k1, lambda_q2, lambda_k2)
    out_a = _diff_attn(lams, diff_norm.reshape(2 * HEAD_DIM, 1), qa, ka_m, va_m.T, ka, vat, batch)

    out_b = _win_attn(win_sink.reshape(-1), qb, kb_m, vb_m.T, kb, vbt, batch)

    out = _out_ffn(h1, out_a, out_b, win_norm, w_out[0].astype(BF16), ffn2_norm,
                   ffn2_w_gate[0].astype(BF16), ffn2_w_up[0].astype(BF16), ffn2_w_down[0].astype(BF16),
                   final_norm.reshape(1, d))
    return out.reshape(batch, seq, d)
```

```python
import functools
import math

import jax
import jax.numpy as jnp
from jax import lax
from jax.experimental import pallas as pl
from jax.experimental.pallas import tpu as pltpu

N_META = 16
HEAD_DIM = 64
DIFF_HEADS = 4
WIN_KV_HEADS = 2
WIN_GROUP = 4
WINDOW = 128
ROPE_THETA = 10000.0
EPS = 1e-6
NEG = -1e30
LAMBDA_INIT = 0.8 - 0.6 * math.exp(-0.3 * 0)
QK_SCALE = HEAD_DIM ** -0.5
LOG2E = math.log2(math.e)

LANES = 128
BF16_SUBLANES = 16
VMEM_LIMIT_ROWWISE = 56 * 1024 * 1024
VMEM_LIMIT_ATTN = 40 * 1024 * 1024

ROW_TILE = 512
MXU_WIDTH = 256
FF_CHUNK = 6 * MXU_WIDTH
DIFF_TQ = 512
DIFF_TK = 1024
WIN_TQ = 128
WIN_SPAN = WIN_TQ + 2 * WINDOW
WIN_QBLOCK = 4096

F32 = jnp.float32
BF16 = jnp.bfloat16
_NT = (((1,), (1,)), ((), ()))


def _rms(x, gain):
    return x * lax.rsqrt(jnp.mean(x * x, axis=-1, keepdims=True) + EPS) * gain


def _rms_split(x, gain):
    return (x * gain).astype(BF16), lax.rsqrt(jnp.mean(x * x, axis=-1, keepdims=True) + EPS)


def _swiglu(xn, wg_ref, wu_ref, wd_ref, row_scale=None):
    d_ff = wg_ref.shape[1]
    acc = None
    for start in range(0, d_ff, FF_CHUNK):
        sl = slice(start, min(start + FF_CHUNK, d_ff))
        g = jnp.dot(xn, wg_ref[:, sl], preferred_element_type=F32)
        u = jnp.dot(xn, wu_ref[:, sl], preferred_element_type=F32)
        if row_scale is not None:
            g = g * row_scale
            u = u * row_scale
        a = (g * jax.nn.sigmoid(g) * u).astype(BF16)
        d = jnp.dot(a, wd_ref[sl, :], preferred_element_type=F32)
        acc = d if acc is None else acc + d
    return acc


def _rope_block(x, cos, sin_signed, first_half):
    partner = jnp.where(first_half, pltpu.roll(x, LANES - 32, 1), pltpu.roll(x, 32, 1))
    return x * cos + partner * sin_signed


def _ffn_in_kernel(x_ref, cos_ref, sin_ref, n1_ref, wg_ref, wu_ref, wd_ref, n2_ref, win_ref,
                   h_ref, qa_ref, ka_ref, va_ref, qb_ref, kb_ref, vb_ref, *, v_transposed):
    x = x_ref[...]
    xg, x_scale = _rms_split(x, n1_ref[...])
    h = x + 0.5 * _swiglu(xg, wg_ref, wu_ref, wd_ref, x_scale)
    h_ref[...] = h
    u, h_scale = _rms_split(h, n2_ref[...])

    rows = x.shape[0]
    cos = cos_ref[...]
    sin_signed = sin_ref[...]
    lane = lax.broadcasted_iota(jnp.int32, (rows, LANES), 1)
    first_half = (lane % HEAD_DIM) < (HEAD_DIM // 2)
    low = lane < HEAD_DIM

    def project(col0, width):
        return jnp.dot(u, win_ref[:, col0:col0 + width], preferred_element_type=F32) * h_scale

    def store_rope(dst_ref, z, scale):
        for i in range(z.shape[1] // LANES):
            blk = _rope_block(z[:, i * LANES:(i + 1) * LANES], cos, sin_signed, first_half)
            if scale != 1.0:
                blk = blk * scale
            dst_ref[:, i * LANES:(i + 1) * LANES] = blk.astype(dst_ref.dtype)

    def store_padded(dst_ref, z):
        swapped = pltpu.roll(z, HEAD_DIM, 1)
        zero = jnp.zeros_like(z)
        parts = (jnp.where(low, z, zero), jnp.where(low, zero, swapped),
                 jnp.where(low, swapped, zero), jnp.where(low, zero, z))
        for i, part in enumerate(parts):
            dst_ref[:, i * LANES:(i + 1) * LANES] = part.astype(dst_ref.dtype)

    def store_padded_transposed(dst_ref, z):
        zt = z.T
        swapped = jnp.concatenate([zt[HEAD_DIM:], zt[:HEAD_DIM]], axis=0)
        top = lax.broadcasted_iota(jnp.int32, zt.shape, 0) < HEAD_DIM
        zero = jnp.zeros_like(zt)
        parts = (jnp.where(top, zt, zero), jnp.where(top, zero, swapped),
                 jnp.where(top, swapped, zero), jnp.where(top, zero, zt))
        for i, part in enumerate(parts):
            dst_ref[i * LANES:(i + 1) * LANES, :] = part.astype(dst_ref.dtype)

    store_rope(qa_ref, project(0, 512), QK_SCALE * LOG2E)
    store_rope(ka_ref, project(512, 512), 1.0)
    store_rope(qb_ref, project(1536, 512), QK_SCALE * LOG2E)
    zkv = project(2048, 2 * LANES)
    store_padded(kb_ref, _rope_block(zkv[:, :LANES], cos, sin_signed, first_half))
    zva = project(1024, 512)
    zvb = zkv[:, LANES:]
    if v_transposed:
        for i in range(zva.shape[1] // LANES):
            va_ref[i * LANES:(i + 1) * LANES, :] = zva[:, i * LANES:(i + 1) * LANES].T.astype(va_ref.dtype)
        store_padded_transposed(vb_ref, zvb)
    else:
        va_ref[...] = zva.astype(va_ref.dtype)
        store_padded(vb_ref, zvb)


def _const_spec(shape):
    zeros = (0,) * len(shape)
    return pl.BlockSpec(shape, lambda *_: zeros, pipeline_mode=pl.Buffered(1))


def _ffn_in(x, cos, sin_signed, n1, wg, wu, wd, n2, win, row_tile, v_transposed):
    rows, d = x.shape
    pos_tiles = cos.shape[0] // row_tile
    row = lambda i: (i, 0)
    out_w = 512
    wide = pl.BlockSpec((row_tile, out_w), row)
    wide_shape = jax.ShapeDtypeStruct((rows, out_w), BF16)
    v_spec = pl.BlockSpec((out_w, row_tile), lambda i: (0, i)) if v_transposed else wide
    v_shape = jax.ShapeDtypeStruct((out_w, rows), BF16) if v_transposed else wide_shape
    return pl.pallas_call(
        functools.partial(_ffn_in_kernel, v_transposed=v_transposed),
        grid=(rows // row_tile,),
        in_specs=[
            pl.BlockSpec((row_tile, d), row),
            pl.BlockSpec((row_tile, LANES), lambda i: (i % pos_tiles, 0)),
            pl.BlockSpec((row_tile, LANES), lambda i: (i % pos_tiles, 0)),
            _const_spec(n1.shape), _const_spec(wg.shape), _const_spec(wu.shape), _const_spec(wd.shape),
            _const_spec(n2.shape), _const_spec(win.shape),
        ],
        out_specs=[pl.BlockSpec((row_tile, d), row), wide, wide, v_spec, wide, wide, v_spec],
        out_shape=[jax.ShapeDtypeStruct((rows, d), F32), wide_shape, wide_shape, v_shape, wide_shape, wide_shape,
                   v_shape],
        compiler_params=pltpu.CompilerParams(dimension_semantics=("parallel",),
                                             vmem_limit_bytes=VMEM_LIMIT_ROWWISE),
        name="ffn_in",
    )(x, cos, sin_signed, n1, wg, wu, wd, n2, win)


def _diff_attn_kernel(lq1_ref, lk1_ref, lq2_ref, lk2_ref, gain_ref, q_ref, km_ref, vmt_ref, k_ref, vt_ref,
                      o_ref, qz_sc, sa_sc, sb_sc, sm_sc, m_sc, acc_sc):
    lam = (jnp.exp(jnp.sum(lq1_ref[...] * lk1_ref[...], keepdims=True))
           - jnp.exp(jnp.sum(lq2_ref[...] * lk2_ref[...], keepdims=True)) + LAMBDA_INIT)
    tq, tk = DIFF_TQ, DIFF_TK
    vdim = vt_ref.shape[0]
    nq = q_ref.shape[0] // tq
    nk = vt_ref.shape[1] // tk

    def with_ones(vt):
        return jnp.concatenate([vt, jnp.ones((BF16_SUBLANES, vt.shape[1]), BF16)], axis=0)

    def load_q(i, slot):
        q = q_ref[pl.ds(pl.multiple_of(i * tq, tq), tq), :]
        lane = lax.broadcasted_iota(jnp.int32, q.shape, 1)
        zero = jnp.zeros_like(q)
        qz_sc[slot, 0] = jnp.where(lane < HEAD_DIM, q, zero)
        qz_sc[slot, 1] = jnp.where(lane < HEAD_DIM, zero, q)

    def produce(t, s_sc, slot):
        kt = k_ref[pl.ds(pl.multiple_of(t * tk, tk), tk), :]
        maxima = []
        for c in range(2):
            s = lax.dot_general(kt, qz_sc[slot, c], _NT, preferred_element_type=F32)
            s_sc[c, :, :tq] = s
            maxima.append(jnp.max(s, axis=0, keepdims=True))
        return tuple(maxima)

    def fold(c, s_ref, s_max, vt):
        m_prev = m_sc[c]
        m_new = jnp.maximum(m_prev, s_max)
        alpha = jnp.exp2(m_prev - m_new)
        m_sc[c] = m_new
        for n0 in range(0, tq, MXU_WIDTH):
            cols = slice(n0, n0 + MXU_WIDTH)
            p = jnp.exp2(s_ref[:, cols] - m_new[:, cols])
            acc_sc[c, :, cols] = (alpha[:, cols] * acc_sc[c, :, cols]
                                  + jnp.dot(vt, p.astype(BF16), preferred_element_type=F32))

    def step(t, s_sc, maxima, t_next, s_next_sc, slot_next):
        vt = with_ones(vt_ref[:, pl.ds(pl.multiple_of(t * tk, tk), tk)])
        stats = []
        for c in range(2):
            m_prev = m_sc[c]
            m_new = jnp.maximum(m_prev, maxima[c])
            stats.append((m_new, jnp.exp2(m_prev - m_new)))
            m_sc[c] = m_new
        pieces = [(c, n0) for c in range(2) for n0 in range(0, tq, MXU_WIDTH)]
        sub = tk // len(pieces)
        next_max = [None, None]
        for i, (c, n0) in enumerate(pieces):
            kt = k_ref[pl.ds(pl.multiple_of(t_next * tk + i * sub, sub), sub), :]
            for cc in range(2):
                s = lax.dot_general(kt, qz_sc[slot_next, cc], _NT, preferred_element_type=F32)
                s_next_sc[cc, i * sub:(i + 1) * sub, :tq] = s
                s_max = jnp.max(s, axis=0, keepdims=True)
                next_max[cc] = s_max if next_max[cc] is None else jnp.maximum(next_max[cc], s_max)
            m_new, alpha = stats[c]
            cols = slice(n0, n0 + MXU_WIDTH)
            p = jnp.exp2(s_sc[c, :, cols] - m_new[:, cols])
            acc_sc[c, :, cols] = (alpha[:, cols] * acc_sc[c, :, cols]
                                  + jnp.dot(vt, p.astype(BF16), preferred_element_type=F32))
        return tuple(next_max)

    def meta_scores(slot):
        for c in range(2):
            sm_sc[c] = lax.dot_general(km_ref[...], qz_sc[slot, c], _NT, preferred_element_type=F32)

    def consume_meta():
        vmt = with_ones(vmt_ref[...])
        for c in range(2):
            fold(c, sm_sc.at[c], jnp.max(sm_sc[c], axis=0, keepdims=True), vmt)

    def finalize(i):
        def normalised(c):
            acc = acc_sc[c, :, :tq]
            return acc[:vdim] * (1.0 / acc[vdim:vdim + 1])

        a = normalised(0) - lam * normalised(1)
        ms = jnp.mean(a * a, axis=0, keepdims=True)
        y = a * lax.rsqrt(ms + EPS) * gain_ref[...] * (1.0 - LAMBDA_INIT)
        o_ref[pl.ds(pl.multiple_of(i * tq, tq), tq), :] = y.T.astype(o_ref.dtype)

    def query_tile(i, maxima_a):
        slot = i % 2
        for c in range(2):
            m_sc[c] = jnp.full(m_sc.shape[1:], NEG, F32)
            acc_sc[c] = jnp.zeros(acc_sc.shape[1:], F32)

        def pair(j, maxima_a):
            t = 2 * j
            maxima_b = step(t, sa_sc, maxima_a, t + 1, sb_sc, slot)
            return step(t + 1, sb_sc, maxima_b, t + 2, sa_sc, slot)

        maxima_a = lax.fori_loop(0, nk // 2 - 1, pair, maxima_a)
        meta_scores(slot)
        maxima_b = step(nk - 2, sa_sc, maxima_a, nk - 1, sb_sc, slot)
        load_q(jnp.minimum(i + 1, nq - 1), 1 - slot)
        consume_meta()
        maxima_next = step(nk - 1, sb_sc, maxima_b, 0, sa_sc, 1 - slot)
        finalize(i)
        return maxima_next

    load_q(0, 0)
    lax.fori_loop(0, nq, query_tile, produce(0, sa_sc, 0))


def _diff_attn(lams, gain_col, qa, ka_meta, vat_meta, ka, vat, batch):
    rows = qa.shape[0]
    seq = rows // batch
    vdim = 2 * HEAD_DIM
    tq, tk = DIFF_TQ, DIFF_TK
    assert (seq // tk) % 2 == 0
    lam_spec = pl.BlockSpec((1, HEAD_DIM), lambda b, h: (0, 0))
    return pl.pallas_call(
        _diff_attn_kernel,
        grid=(batch, DIFF_HEADS),
        in_specs=[
            lam_spec, lam_spec, lam_spec, lam_spec,
            pl.BlockSpec((vdim, 1), lambda b, h: (0, 0)),
            pl.BlockSpec((seq, LANES), lambda b, h: (b, h)),
            pl.BlockSpec((N_META, LANES), lambda b, h: (0, h)),
            pl.BlockSpec((vdim, N_META), lambda b, h: (h, 0)),
            pl.BlockSpec((seq, LANES), lambda b, h: (b, h)),
            pl.BlockSpec((vdim, seq), lambda b, h: (h, b)),
        ],
        out_specs=pl.BlockSpec((seq, LANES), lambda b, h: (b, h)),
        out_shape=jax.ShapeDtypeStruct((rows, DIFF_HEADS * vdim), BF16),
        scratch_shapes=[pltpu.VMEM((2, 2, tq, LANES), BF16),
                        pltpu.VMEM((2, tk, tq + LANES), F32), pltpu.VMEM((2, tk, tq + LANES), F32),
                        pltpu.VMEM((2, N_META, tq), F32),
                        pltpu.VMEM((2, 1, tq), F32),
                        pltpu.VMEM((2, vdim + BF16_SUBLANES, tq + LANES), F32)],
        compiler_params=pltpu.CompilerParams(dimension_semantics=("parallel", "parallel"),
                                             vmem_limit_bytes=VMEM_LIMIT_ATTN),
        name="diff_attn",
    )(*lams, gain_col, qa, ka_meta, vat_meta, ka, vat)


def _win_attn_kernel(sink_ref, q_ref, km_ref, vmt_ref, k_ref, vt_ref, o_ref,
                     delta_sc, sa_sc, sb_sc, sma_sc, smb_sc):
    g = pl.program_id(1)
    blk = pl.program_id(2)
    seq = k_ref.shape[0]
    tq, span = WIN_TQ, WIN_SPAN
    n = 2 * tq
    nt = q_ref.shape[0] // tq
    delta_sc[...] = (lax.broadcasted_iota(jnp.int32, (span, tq), 0)
                     - lax.broadcasted_iota(jnp.int32, (span, tq), 1))
    first_pair = lax.broadcasted_iota(jnp.int32, (1, n), 1) < tq
    sinks = [jnp.where(first_pair, sink_ref[g * WIN_GROUP + par], sink_ref[g * WIN_GROUP + 2 + par]) * LOG2E
             for par in range(2)]

    ones_row = (HEAD_DIM, 0)
    top_half = lax.broadcasted_iota(jnp.int32, (LANES, n), 0) < HEAD_DIM

    def key_start(i):
        t0 = blk * (nt * tq) + i * tq
        return t0, pl.multiple_of(jnp.clip(t0 - WINDOW, 0, seq - span), WINDOW)

    def produce_pieces(i, s_sc, sm_sc):
        t0, ks = key_start(i)
        off = ks - t0
        rows = pl.ds(pl.multiple_of(i * tq, tq), tq)

        def piece(par):
            delta = delta_sc[...]
            valid = (delta >= -WINDOW - off) & (delta <= WINDOW - off)
            q_cat = jnp.concatenate([q_ref[rows, :LANES], q_ref[rows, LANES:]], axis=0)
            part = slice(par * LANES, (par + 1) * LANES)
            k_cat = jnp.concatenate([k_ref[pl.ds(ks, span), part], km_ref[:, part]], axis=0)
            s_all = lax.dot_general(k_cat, q_cat, _NT, preferred_element_type=F32)
            s = s_all[:span]
            s = jnp.concatenate([jnp.where(valid, s[:, :tq], NEG), jnp.where(valid, s[:, tq:], NEG)], axis=1)
            sm = s_all[span:]
            s_sc[par] = s
            sm_sc[par] = sm
            return jnp.maximum(jnp.maximum(jnp.max(s, axis=0, keepdims=True),
                                           jnp.max(sm, axis=0, keepdims=True)), sinks[par])

        return [functools.partial(piece, par) for par in range(2)]

    def consume_pieces(i, s_sc, sm_sc, maxima):
        _, ks = key_start(i)

        def with_ones(vt, par):
            r = ones_row[par]
            ones = jnp.ones((BF16_SUBLANES, vt.shape[1]), BF16)
            return jnp.concatenate(([vt[:r]] if r else []) + [ones, vt[r + BF16_SUBLANES:]], axis=0)

        def piece(par):
            part = slice(par * LANES, (par + 1) * LANES)
            m = maxima[par]
            p = jnp.exp2(s_sc[par] - m)
            pm = jnp.exp2(sm_sc[par] - m)
            o = (jnp.dot(with_ones(vt_ref[part, pl.ds(ks, span)], par), p.astype(BF16),
                         preferred_element_type=F32)
                 + jnp.dot(with_ones(vmt_ref[part, :], par), pm.astype(BF16), preferred_element_type=F32))
            l = o[ones_row[par]:ones_row[par] + 1] + jnp.exp2(sinks[par] - m)
            return o * (1.0 / l)

        def store(outs):
            o_t = jnp.where(top_half, outs[0], outs[1])
            rows = pl.ds(pl.multiple_of(i * tq, tq), tq)
            for pair in range(WIN_GROUP // 2):
                o_ref[rows, pair * LANES:(pair + 1) * LANES] = (
                    o_t[:, pair * tq:(pair + 1) * tq].T.astype(o_ref.dtype))

        return [functools.partial(piece, par) for par in range(2)], store

    def produce(i, s_sc, sm_sc):
        return tuple(piece() for piece in produce_pieces(i, s_sc, sm_sc))

    def consume(i, s_sc, sm_sc, maxima):
        pieces, store = consume_pieces(i, s_sc, sm_sc, maxima)
        store([piece() for piece in pieces])

    def step(i, s_sc, sm_sc, maxima, i_next, s_next_sc, sm_next_sc):
        makers = produce_pieces(i_next, s_next_sc, sm_next_sc)
        users, store = consume_pieces(i, s_sc, sm_sc, maxima)
        next_maxima, outs = [], []
        for make, use in zip(makers, users):
            next_maxima.append(make())
            outs.append(use())
        store(outs)
        return tuple(next_maxima)

    def pair_of_tiles(j, maxima_a):
        i = 2 * j
        maxima_b = step(i, sa_sc, sma_sc, maxima_a, i + 1, sb_sc, smb_sc)
        return step(i + 1, sb_sc, smb_sc, maxima_b, i + 2, sa_sc, sma_sc)

    maxima_a = lax.fori_loop(0, nt // 2 - 1, pair_of_tiles, produce(0, sa_sc, sma_sc), unroll=5)
    maxima_b = step(nt - 2, sa_sc, sma_sc, maxima_a, nt - 1, sb_sc, smb_sc)
    consume(nt - 1, sb_sc, smb_sc, maxima_b)


def _win_attn(sink, qb, kb_meta, vbt_meta, kb, vbt, batch):
    rows = qb.shape[0]
    seq = rows // batch
    nblk = seq // WIN_QBLOCK
    assert (WIN_QBLOCK // WIN_TQ) % 2 == 0
    width = 2 * LANES
    n = 2 * WIN_TQ
    return pl.pallas_call(
        _win_attn_kernel,
        grid=(batch, WIN_KV_HEADS, nblk),
        in_specs=[
            pl.BlockSpec(memory_space=pltpu.SMEM),
            pl.BlockSpec((WIN_QBLOCK, width), lambda b, g, i: (b * nblk + i, g)),
            pl.BlockSpec((N_META, width), lambda b, g, i: (0, g)),
            pl.BlockSpec((width, N_META), lambda b, g, i: (g, 0)),
            pl.BlockSpec((seq, width), lambda b, g, i: (b, g)),
            pl.BlockSpec((width, seq), lambda b, g, i: (g, b)),
        ],
        out_specs=pl.BlockSpec((WIN_QBLOCK, width), lambda b, g, i: (b * nblk + i, g)),
        out_shape=jax.ShapeDtypeStruct((rows, WIN_KV_HEADS * width), BF16),
        scratch_shapes=[pltpu.VMEM((WIN_SPAN, WIN_TQ), jnp.int32),
                        pltpu.VMEM((2, WIN_SPAN, n), F32), pltpu.VMEM((2, WIN_SPAN, n), F32),
                        pltpu.VMEM((2, N_META, n), F32), pltpu.VMEM((2, N_META, n), F32)],
        compiler_params=pltpu.CompilerParams(dimension_semantics=("parallel", "parallel", "arbitrary"),
                                             vmem_limit_bytes=VMEM_LIMIT_ATTN),
        name="win_attn",
    )(sink, qb, kb_meta, vbt_meta, kb, vbt)


def _out_ffn_kernel(h_ref, oa_ref, ob_ref, wn_ref, wo_ref, n_ref, wg_ref, wu_ref, wd_ref, fn_ref, out_ref):
    half = oa_ref.shape[1]
    ob, ob_scale = _rms_split(ob_ref[...].astype(F32), wn_ref[...])
    mix = (jnp.dot(oa_ref[...], wo_ref[:half, :], preferred_element_type=F32)
           + jnp.dot(ob, wo_ref[half:, :], preferred_element_type=F32) * ob_scale)
    h = h_ref[...] + mix
    hg, h_scale = _rms_split(h, n_ref[...])
    h = h + 0.5 * _swiglu(hg, wg_ref, wu_ref, wd_ref, h_scale)
    out_ref[...] = _rms(h, fn_ref[...])


def _out_ffn(h1, oa, ob, wn, wo, n, wg, wu, wd, fn):
    rows, d = h1.shape
    half = oa.shape[1]
    row = lambda i: (i, 0)
    return pl.pallas_call(
        _out_ffn_kernel,
        grid=(rows // ROW_TILE,),
        in_specs=[
            pl.BlockSpec((ROW_TILE, d), row), pl.BlockSpec((ROW_TILE, half), row), pl.BlockSpec((ROW_TILE, half), row),
            _const_spec(wn.shape), _const_spec(wo.shape), _const_spec(n.shape),
            _const_spec(wg.shape), _const_spec(wu.shape), _const_spec(wd.shape), _const_spec(fn.shape),
        ],
        out_specs=pl.BlockSpec((ROW_TILE, d), row),
        out_shape=jax.ShapeDtypeStruct((rows, d), F32),
        compiler_params=pltpu.CompilerParams(dimension_semantics=("parallel",),
                                             vmem_limit_bytes=VMEM_LIMIT_ROWWISE),
        name="out_ffn",
    )(h1, oa, ob, wn, wo, n, wg, wu, wd, fn)


def _rope_tables(length):
    pos = jnp.arange(length, dtype=F32)
    inv = ROPE_THETA ** (-jnp.arange(0, HEAD_DIM, 2, dtype=F32) / HEAD_DIM)
    ang = pos[:, None] * inv[None, :]
    cos, sin = jnp.cos(ang), jnp.sin(ang)
    return jnp.tile(cos, (1, 4)), jnp.tile(jnp.concatenate([-sin, sin], axis=1), (1, 2))


def kernel(x, meta_tokens, ffn1_norm, ffn1_w_gate, ffn1_w_up, ffn1_w_down, mix_norm, w_in, lambda_q1, lambda_k1, lambda_q2, lambda_k2, diff_norm, win_sink, win_norm, w_out, ffn2_norm, ffn2_w_gate, ffn2_w_up, ffn2_w_down, final_norm):
    batch, seq, d = x.shape
    assert ffn1_norm.shape[0] == 1, "single layer only"
    assert seq % ROW_TILE == 0 and seq % DIFF_TQ == 0 and seq % DIFF_TK == 0 and seq % WIN_QBLOCK == 0

    cos, sin_signed = _rope_tables(N_META + seq)
    ffn_in_weights = (ffn1_norm, ffn1_w_gate[0].astype(BF16), ffn1_w_up[0].astype(BF16),
                      ffn1_w_down[0].astype(BF16), mix_norm, w_in[0].astype(BF16))

    meta = _ffn_in(meta_tokens.astype(x.dtype), cos[:N_META], sin_signed[:N_META], *ffn_in_weights, N_META,
                   v_transposed=False)
    real = _ffn_in(x.reshape(batch * seq, d), cos[N_META:], sin_signed[N_META:], *ffn_in_weights, ROW_TILE,
                   v_transposed=True)
    _, _, ka_m, va_m, _, kb_m, vb_m = meta
    h1, qa, ka, vat, qb, kb, vbt = real

    lams = (lambda_q1, lambda_k1, lambda_q2, lambda_k2)
    out_a = _diff_attn(lams, diff_norm.reshape(2 * HEAD_DIM, 1), qa, ka_m, va_m.T, ka, vat, batch)

    out_b = _win_attn(win_sink.reshape(-1), qb, kb_m, vb_m.T, kb, vbt, batch)

    out = _out_ffn(h1, out_a, out_b, win_norm, w_out[0].astype(BF16), ffn2_norm,
                   ffn2_w_gate[0].astype(BF16), ffn2_w_up[0].astype(BF16), ffn2_w_down[0].astype(BF16),
                   final_norm.reshape(1, d))
    return out.reshape(batch, seq, d)
```

```python
import functools
import math

import jax
import jax.numpy as jnp
from jax import lax
from jax.experimental import pallas as pl
from jax.experimental.pallas import tpu as pltpu

N_META = 16
HEAD_DIM = 64
DIFF_HEADS = 4
WIN_KV_HEADS = 2
WIN_GROUP = 4
WINDOW = 128
ROPE_THETA = 10000.0
EPS = 1e-6
NEG = -1e30
LAMBDA_INIT = 0.8 - 0.6 * math.exp(-0.3 * 0)
QK_SCALE = HEAD_DIM ** -0.5
LOG2E = math.log2(math.e)

LANES = 128
BF16_SUBLANES = 16
VMEM_LIMIT_ROWWISE = 56 * 1024 * 1024
VMEM_LIMIT_ATTN = 40 * 1024 * 1024

ROW_TILE = 512
MXU_WIDTH = 256
FF_CHUNK = 6 * MXU_WIDTH
DIFF_TQ = 512
DIFF_TK = 1024
WIN_TQ = 128
WIN_SPAN = WIN_TQ + 2 * WINDOW
WIN_QBLOCK = 4096

F32 = jnp.float32
BF16 = jnp.bfloat16
_NT = (((1,), (1,)), ((), ()))


def _rms(x, gain):
    return x * lax.rsqrt(jnp.mean(x * x, axis=-1, keepdims=True) + EPS) * gain


def _rms_split(x, gain):
    return (x * gain).astype(BF16), lax.rsqrt(jnp.mean(x * x, axis=-1, keepdims=True) + EPS)


def _swiglu(xn, wg_ref, wu_ref, wd_ref, row_scale=None):
    d_ff = wg_ref.shape[1]
    acc = None
    for start in range(0, d_ff, FF_CHUNK):
        sl = slice(start, min(start + FF_CHUNK, d_ff))
        g = jnp.dot(xn, wg_ref[:, sl], preferred_element_type=F32)
        u = jnp.dot(xn, wu_ref[:, sl], preferred_element_type=F32)
        if row_scale is not None:
            g = g * row_scale
            u = u * row_scale
        a = (g * jax.nn.sigmoid(g) * u).astype(BF16)
        d = jnp.dot(a, wd_ref[sl, :], preferred_element_type=F32)
        acc = d if acc is None else acc + d
    return acc


def _rope_block(x, cos, sin_signed, first_half):
    partner = jnp.where(first_half, pltpu.roll(x, LANES - 32, 1), pltpu.roll(x, 32, 1))
    return x * cos + partner * sin_signed


def _ffn_in_kernel(x_ref, cos_ref, sin_ref, n1_ref, wg_ref, wu_ref, wd_ref, n2_ref, win_ref,
                   h_ref, qa_ref, ka_ref, va_ref, qb_ref, kb_ref, vb_ref, *, v_transposed):
    x = x_ref[...]
    xg, x_scale = _rms_split(x, n1_ref[...])
    h = x + 0.5 * _swiglu(xg, wg_ref, wu_ref, wd_ref, x_scale)
    h_ref[...] = h
    u, h_scale = _rms_split(h, n2_ref[...])

    rows = x.shape[0]
    cos = cos_ref[...]
    sin_signed = sin_ref[...]
    lane = lax.broadcasted_iota(jnp.int32, (rows, LANES), 1)
    first_half = (lane % HEAD_DIM) < (HEAD_DIM // 2)
    low = lane < HEAD_DIM

    def project(col0, width):
        return jnp.dot(u, win_ref[:, col0:col0 + width], preferred_element_type=F32) * h_scale

    def store_rope(dst_ref, z, scale):
        for i in range(z.shape[1] // LANES):
            blk = _rope_block(z[:, i * LANES:(i + 1) * LANES], cos, sin_signed, first_half)
            if scale != 1.0:
                blk = blk * scale
            dst_ref[:, i * LANES:(i + 1) * LANES] = blk.astype(dst_ref.dtype)

    def store_padded(dst_ref, z):
        swapped = pltpu.roll(z, HEAD_DIM, 1)
        zero = jnp.zeros_like(z)
        parts = (jnp.where(low, z, zero), jnp.where(low, zero, swapped),
                 jnp.where(low, swapped, zero), jnp.where(low, zero, z))
        for i, part in enumerate(parts):
            dst_ref[:, i * LANES:(i + 1) * LANES] = part.astype(dst_ref.dtype)

    def store_padded_transposed(dst_ref, z):
        zt = z.T
        swapped = jnp.concatenate([zt[HEAD_DIM:], zt[:HEAD_DIM]], axis=0)
        top = lax.broadcasted_iota(jnp.int32, zt.shape, 0) < HEAD_DIM
        zero = jnp.zeros_like(zt)
        parts = (jnp.where(top, zt, zero), jnp.where(top, zero, swapped),
                 jnp.where(top, swapped, zero), jnp.where(top, zero, zt))
        for i, part in enumerate(parts):
            dst_ref[i * LANES:(i + 1) * LANES, :] = part.astype(dst_ref.dtype)

    store_rope(qa_ref, project(0, 512), QK_SCALE * LOG2E)
    store_rope(ka_ref, project(512, 512), 1.0)
    store_rope(qb_ref, project(1536, 512), QK_SCALE * LOG2E)
    zkv = project(2048, 2 * LANES)
    store_padded(kb_ref, _rope_block(zkv[:, :LANES], cos, sin_signed, first_half))
    zva = project(1024, 512)
    zvb = zkv[:, LANES:]
    if v_transposed:
        for i in range(zva.shape[1] // LANES):
            va_ref[i * LANES:(i + 1) * LANES, :] = zva[:, i * LANES:(i + 1) * LANES].T.astype(va_ref.dtype)
        store_padded_transposed(vb_ref, zvb)
    else:
        va_ref[...] = zva.astype(va_ref.dtype)
        store_padded(vb_ref, zvb)


def _const_spec(shape):
    zeros = (0,) * len(shape)
    return pl.BlockSpec(shape, lambda *_: zeros, pipeline_mode=pl.Buffered(1))


def _ffn_in(x, cos, sin_signed, n1, wg, wu, wd, n2, win, row_tile, v_transposed):
    rows, d = x.shape
    pos_tiles = cos.shape[0] // row_tile
    row = lambda i: (i, 0)
    out_w = 512
    wide = pl.BlockSpec((row_tile, out_w), row)
    wide_shape = jax.ShapeDtypeStruct((rows, out_w), BF16)
    v_spec = pl.BlockSpec((out_w, row_tile), lambda i: (0, i)) if v_transposed else wide
    v_shape = jax.ShapeDtypeStruct((out_w, rows), BF16) if v_transposed else wide_shape
    return pl.pallas_call(
        functools.partial(_ffn_in_kernel, v_transposed=v_transposed),
        grid=(rows // row_tile,),
        in_specs=[
            pl.BlockSpec((row_tile, d), row),
            pl.BlockSpec((row_tile, LANES), lambda i: (i % pos_tiles, 0)),
            pl.BlockSpec((row_tile, LANES), lambda i: (i % pos_tiles, 0)),
            _const_spec(n1.shape), _const_spec(wg.shape), _const_spec(wu.shape), _const_spec(wd.shape),
            _const_spec(n2.shape), _const_spec(win.shape),
        ],
        out_specs=[pl.BlockSpec((row_tile, d), row), wide, wide, v_spec, wide, wide, v_spec],
        out_shape=[jax.ShapeDtypeStruct((rows, d), F32), wide_shape, wide_shape, v_shape, wide_shape, wide_shape,
                   v_shape],
        compiler_params=pltpu.CompilerParams(dimension_semantics=("parallel",),
                                             vmem_limit_bytes=VMEM_LIMIT_ROWWISE),
        name="ffn_in",
    )(x, cos, sin_signed, n1, wg, wu, wd, n2, win)


def _diff_attn_kernel(lq1_ref, lk1_ref, lq2_ref, lk2_ref, gain_ref, q_ref, km_ref, vmt_ref, k_ref, vt_ref,
                      o_ref, qz_sc, sa_sc, sb_sc, sm_sc, m_sc, acc_sc):
    lam = (jnp.exp(jnp.sum(lq1_ref[...] * lk1_ref[...], keepdims=True))
           - jnp.exp(jnp.sum(lq2_ref[...] * lk2_ref[...], keepdims=True)) + LAMBDA_INIT)
    tq, tk = DIFF_TQ, DIFF_TK
    vdim = vt_ref.shape[0]
    nq = q_ref.shape[0] // tq
    nk = vt_ref.shape[1] // tk

    def with_ones(vt):
        return jnp.concatenate([vt, jnp.ones((BF16_SUBLANES, vt.shape[1]), BF16)], axis=0)

    def load_q(i, slot):
        q = q_ref[pl.ds(pl.multiple_of(i * tq, tq), tq), :]
        lane = lax.broadcasted_iota(jnp.int32, q.shape, 1)
        zero = jnp.zeros_like(q)
        qz_sc[slot, 0] = jnp.where(lane < HEAD_DIM, q, zero)
        qz_sc[slot, 1] = jnp.where(lane < HEAD_DIM, zero, q)

    def produce(t, s_sc, slot):
        kt = k_ref[pl.ds(pl.multiple_of(t * tk, tk), tk), :]
        maxima = []
        for c in range(2):
            s = lax.dot_general(kt, qz_sc[slot, c], _NT, preferred_element_type=F32)
            s_sc[c] = s
            maxima.append(jnp.max(s, axis=0, keepdims=True))
        return tuple(maxima)

    def fold(c, s_ref, s_max, vt):
        m_prev = m_sc[c]
        m_new = jnp.maximum(m_prev, s_max)
        alpha = jnp.exp2(m_prev - m_new)
        m_sc[c] = m_new
        for n0 in range(0, tq, MXU_WIDTH):
            cols = slice(n0, n0 + MXU_WIDTH)
            p = jnp.exp2(s_ref[:, cols] - m_new[:, cols])
            acc_sc[c, :, cols] = (alpha[:, cols] * acc_sc[c, :, cols]
                                  + jnp.dot(vt, p.astype(BF16), preferred_element_type=F32))

    def step(t, s_sc, maxima, t_next, s_next_sc, slot_next):
        vt = with_ones(vt_ref[:, pl.ds(pl.multiple_of(t * tk, tk), tk)])
        stats = []
        for c in range(2):
            m_prev = m_sc[c]
            m_new = jnp.maximum(m_prev, maxima[c])
            stats.append((m_new, jnp.exp2(m_prev - m_new)))
            m_sc[c] = m_new
        kt_next = k_ref[pl.ds(pl.multiple_of(t_next * tk, tk), tk), :]
        next_max = [[], []]
        for c in range(2):
            m_new, alpha = stats[c]
            for n0 in range(0, tq, MXU_WIDTH):
                cols = slice(n0, n0 + MXU_WIDTH)
                s = lax.dot_general(kt_next, qz_sc[slot_next, c, cols, :], _NT, preferred_element_type=F32)
                s_next_sc[c, :, cols] = s
                next_max[c].append(jnp.max(s, axis=0, keepdims=True))
                p = jnp.exp2(s_sc[c, :, cols] - m_new[:, cols])
                acc_sc[c, :, cols] = (alpha[:, cols] * acc_sc[c, :, cols]
                                      + jnp.dot(vt, p.astype(BF16), preferred_element_type=F32))
        return tuple(jnp.concatenate(parts, axis=1) for parts in next_max)

    def meta_scores(slot):
        for c in range(2):
            sm_sc[c] = lax.dot_general(km_ref[...], qz_sc[slot, c], _NT, preferred_element_type=F32)

    def consume_meta():
        vmt = with_ones(vmt_ref[...])
        for c in range(2):
            fold(c, sm_sc.at[c], jnp.max(sm_sc[c], axis=0, keepdims=True), vmt)

    def finalize(i):
        def normalised(c):
            acc = acc_sc[c]
            return acc[:vdim] * (1.0 / acc[vdim:vdim + 1])

        a = normalised(0) - lam * normalised(1)
        ms = jnp.mean(a * a, axis=0, keepdims=True)
        y = a * lax.rsqrt(ms + EPS) * gain_ref[...] * (1.0 - LAMBDA_INIT)
        o_ref[pl.ds(pl.multiple_of(i * tq, tq), tq), :] = y.T.astype(o_ref.dtype)

    def query_tile(i, maxima_a):
        slot = i % 2
        for c in range(2):
            m_sc[c] = jnp.full(m_sc.shape[1:], NEG, F32)
            acc_sc[c] = jnp.zeros(acc_sc.shape[1:], F32)

        def pair(j, maxima_a):
            t = 2 * j
            maxima_b = step(t, sa_sc, maxima_a, t + 1, sb_sc, slot)
            return step(t + 1, sb_sc, maxima_b, t + 2, sa_sc, slot)

        maxima_a = lax.fori_loop(0, nk // 2 - 1, pair, maxima_a)
        meta_scores(slot)
        maxima_b = step(nk - 2, sa_sc, maxima_a, nk - 1, sb_sc, slot)
        load_q(jnp.minimum(i + 1, nq - 1), 1 - slot)
        maxima_next = step(nk - 1, sb_sc, maxima_b, 0, sa_sc, 1 - slot)
        consume_meta()
        finalize(i)
        return maxima_next

    load_q(0, 0)
    lax.fori_loop(0, nq, query_tile, produce(0, sa_sc, 0))


def _diff_attn(lams, gain_col, qa, ka_meta, vat_meta, ka, vat, batch):
    rows = qa.shape[0]
    seq = rows // batch
    vdim = 2 * HEAD_DIM
    tq, tk = DIFF_TQ, DIFF_TK
    assert (seq // tk) % 2 == 0
    lam_spec = pl.BlockSpec((1, HEAD_DIM), lambda b, h: (0, 0))
    return pl.pallas_call(
        _diff_attn_kernel,
        grid=(batch, DIFF_HEADS),
        in_specs=[
            lam_spec, lam_spec, lam_spec, lam_spec,
            pl.BlockSpec((vdim, 1), lambda b, h: (0, 0)),
            pl.BlockSpec((seq, LANES), lambda b, h: (b, h)),
            pl.BlockSpec((N_META, LANES), lambda b, h: (0, h)),
            pl.BlockSpec((vdim, N_META), lambda b, h: (h, 0)),
            pl.BlockSpec((seq, LANES), lambda b, h: (b, h)),
            pl.BlockSpec((vdim, seq), lambda b, h: (h, b)),
        ],
        out_specs=pl.BlockSpec((seq, LANES), lambda b, h: (b, h)),
        out_shape=jax.ShapeDtypeStruct((rows, DIFF_HEADS * vdim), BF16),
        scratch_shapes=[pltpu.VMEM((2, 2, tq, LANES), BF16),
                        pltpu.VMEM((2, tk, tq), F32), pltpu.VMEM((2, tk, tq), F32),
                        pltpu.VMEM((2, N_META, tq), F32),
                        pltpu.VMEM((2, 1, tq), F32),
                        pltpu.VMEM((2, vdim + BF16_SUBLANES, tq), F32)],
        compiler_params=pltpu.CompilerParams(dimension_semantics=("parallel", "parallel"),
                                             vmem_limit_bytes=VMEM_LIMIT_ATTN),
        name="diff_attn",
    )(*lams, gain_col, qa, ka_meta, vat_meta, ka, vat)


def _win_attn_kernel(sink_ref, q_ref, km_ref, vmt_ref, k_ref, vt_ref, o_ref,
                     delta_sc, sa_sc, sb_sc, sma_sc, smb_sc):
    g = pl.program_id(1)
    blk = pl.program_id(2)
    seq = k_ref.shape[0]
    tq, span = WIN_TQ, WIN_SPAN
    n = 2 * tq
    nt = q_ref.shape[0] // tq
    delta_sc[...] = (lax.broadcasted_iota(jnp.int32, (span, tq), 0)
                     - lax.broadcasted_iota(jnp.int32, (span, tq), 1))
    first_pair = lax.broadcasted_iota(jnp.int32, (1, n), 1) < tq
    sinks = [jnp.where(first_pair, sink_ref[g * WIN_GROUP + par], sink_ref[g * WIN_GROUP + 2 + par]) * LOG2E
             for par in range(2)]

    ones_row = (HEAD_DIM, 0)
    top_half = lax.broadcasted_iota(jnp.int32, (LANES, n), 0) < HEAD_DIM

    def key_start(i):
        t0 = blk * (nt * tq) + i * tq
        return t0, pl.multiple_of(jnp.clip(t0 - WINDOW, 0, seq - span), WINDOW)

    def produce_pieces(i, s_sc, sm_sc):
        t0, ks = key_start(i)
        off = ks - t0
        rows = pl.ds(pl.multiple_of(i * tq, tq), tq)

        def piece(par):
            delta = delta_sc[...]
            valid = (delta >= -WINDOW - off) & (delta <= WINDOW - off)
            q_cat = jnp.concatenate([q_ref[rows, :LANES], q_ref[rows, LANES:]], axis=0)
            part = slice(par * LANES, (par + 1) * LANES)
            k_cat = jnp.concatenate([k_ref[pl.ds(ks, span), part], km_ref[:, part]], axis=0)
            s_all = lax.dot_general(k_cat, q_cat, _NT, preferred_element_type=F32)
            s = s_all[:span]
            s = jnp.concatenate([jnp.where(valid, s[:, :tq], NEG), jnp.where(valid, s[:, tq:], NEG)], axis=1)
            sm = s_all[span:]
            s_sc[par] = s
            sm_sc[par] = sm
            return jnp.maximum(jnp.maximum(jnp.max(s, axis=0, keepdims=True),
                                           jnp.max(sm, axis=0, keepdims=True)), sinks[par])

        return [functools.partial(piece, par) for par in range(2)]

    def consume_pieces(i, s_sc, sm_sc, maxima):
        _, ks = key_start(i)

        def with_ones(vt, par):
            r = ones_row[par]
            ones = jnp.ones((BF16_SUBLANES, vt.shape[1]), BF16)
            return jnp.concatenate(([vt[:r]] if r else []) + [ones, vt[r + BF16_SUBLANES:]], axis=0)

        def piece(par):
            part = slice(par * LANES, (par + 1) * LANES)
            m = maxima[par]
            p = jnp.exp2(s_sc[par] - m)
            pm = jnp.exp2(sm_sc[par] - m)
            o = (jnp.dot(with_ones(vt_ref[part, pl.ds(ks, span)], par), p.astype(BF16),
                         preferred_element_type=F32)
                 + jnp.dot(with_ones(vmt_ref[part, :], par), pm.astype(BF16), preferred_element_type=F32))
            l = o[ones_row[par]:ones_row[par] + 1] + jnp.exp2(sinks[par] - m)
            return o * (1.0 / l)

        def store(outs):
            o_t = jnp.where(top_half, outs[0], outs[1])
            rows = pl.ds(pl.multiple_of(i * tq, tq), tq)
            for pair in range(WIN_GROUP // 2):
                o_ref[rows, pair * LANES:(pair + 1) * LANES] = (
                    o_t[:, pair * tq:(pair + 1) * tq].T.astype(o_ref.dtype))

        return [functools.partial(piece, par) for par in range(2)], store

    def produce(i, s_sc, sm_sc):
        return tuple(piece() for piece in produce_pieces(i, s_sc, sm_sc))

    def consume(i, s_sc, sm_sc, maxima):
        pieces, store = consume_pieces(i, s_sc, sm_sc, maxima)
        store([piece() for piece in pieces])

    def step(i, s_sc, sm_sc, maxima, i_next, s_next_sc, sm_next_sc):
        makers = produce_pieces(i_next, s_next_sc, sm_next_sc)
        users, store = consume_pieces(i, s_sc, sm_sc, maxima)
        next_maxima, outs = [], []
        for make, use in zip(makers, users):
            next_maxima.append(make())
            outs.append(use())
        store(outs)
        return tuple(next_maxima)

    def pair_of_tiles(j, maxima_a):
        i = 2 * j
        maxima_b = step(i, sa_sc, sma_sc, maxima_a, i + 1, sb_sc, smb_sc)
        return step(i + 1, sb_sc, smb_sc, maxima_b, i + 2, sa_sc, sma_sc)

    maxima_a = lax.fori_loop(0, nt // 2 - 1, pair_of_tiles, produce(0, sa_sc, sma_sc), unroll=5)
    maxima_b = step(nt - 2, sa_sc, sma_sc, maxima_a, nt - 1, sb_sc, smb_sc)
    consume(nt - 1, sb_sc, smb_sc, maxima_b)


def _win_attn(sink, qb, kb_meta, vbt_meta, kb, vbt, batch):
    rows = qb.shape[0]
    seq = rows // batch
    nblk = seq // WIN_QBLOCK
    assert (WIN_QBLOCK // WIN_TQ) % 2 == 0
    width = 2 * LANES
    n = 2 * WIN_TQ
    return pl.pallas_call(
        _win_attn_kernel,
        grid=(batch, WIN_KV_HEADS, nblk),
        in_specs=[
            pl.BlockSpec(memory_space=pltpu.SMEM),
            pl.BlockSpec((WIN_QBLOCK, width), lambda b, g, i: (b * nblk + i, g)),
            pl.BlockSpec((N_META, width), lambda b, g, i: (0, g)),
            pl.BlockSpec((width, N_META), lambda b, g, i: (g, 0)),
            pl.BlockSpec((seq, width), lambda b, g, i: (b, g)),
            pl.BlockSpec((width, seq), lambda b, g, i: (g, b)),
        ],
        out_specs=pl.BlockSpec((WIN_QBLOCK, width), lambda b, g, i: (b * nblk + i, g)),
        out_shape=jax.ShapeDtypeStruct((rows, WIN_KV_HEADS * width), BF16),
        scratch_shapes=[pltpu.VMEM((WIN_SPAN, WIN_TQ), jnp.int32),
                        pltpu.VMEM((2, WIN_SPAN, n), F32), pltpu.VMEM((2, WIN_SPAN, n), F32),
                        pltpu.VMEM((2, N_META, n), F32), pltpu.VMEM((2, N_META, n), F32)],
        compiler_params=pltpu.CompilerParams(dimension_semantics=("parallel", "parallel", "arbitrary"),
                                             vmem_limit_bytes=VMEM_LIMIT_ATTN),
        name="win_attn",
    )(sink, qb, kb_meta, vbt_meta, kb, vbt)


def _out_ffn_kernel(h_ref, oa_ref, ob_ref, wn_ref, wo_ref, n_ref, wg_ref, wu_ref, wd_ref, fn_ref, out_ref):
    half = oa_ref.shape[1]
    ob, ob_scale = _rms_split(ob_ref[...].astype(F32), wn_ref[...])
    mix = (jnp.dot(oa_ref[...], wo_ref[:half, :], preferred_element_type=F32)
           + jnp.dot(ob, wo_ref[half:, :], preferred_element_type=F32) * ob_scale)
    h = h_ref[...] + mix
    hg, h_scale = _rms_split(h, n_ref[...])
    h = h + 0.5 * _swiglu(hg, wg_ref, wu_ref, wd_ref, h_scale)
    out_ref[...] = _rms(h, fn_ref[...])


def _out_ffn(h1, oa, ob, wn, wo, n, wg, wu, wd, fn):
    rows, d = h1.shape
    half = oa.shape[1]
    row = lambda i: (i, 0)
    return pl.pallas_call(
        _out_ffn_kernel,
        grid=(rows // ROW_TILE,),
        in_specs=[
            pl.BlockSpec((ROW_TILE, d), row), pl.BlockSpec((ROW_TILE, half), row), pl.BlockSpec((ROW_TILE, half), row),
            _const_spec(wn.shape), _const_spec(wo.shape), _const_spec(n.shape),
            _const_spec(wg.shape), _const_spec(wu.shape), _const_spec(wd.shape), _const_spec(fn.shape),
        ],
        out_specs=pl.BlockSpec((ROW_TILE, d), row),
        out_shape=jax.ShapeDtypeStruct((rows, d), F32),
        compiler_params=pltpu.CompilerParams(dimension_semantics=("parallel",),
                                             vmem_limit_bytes=VMEM_LIMIT_ROWWISE),
        name="out_ffn",
    )(h1, oa, ob, wn, wo, n, wg, wu, wd, fn)


def _rope_tables(length):
    pos = jnp.arange(length, dtype=F32)
    inv = ROPE_THETA ** (-jnp.arange(0, HEAD_DIM, 2, dtype=F32) / HEAD_DIM)
    ang = pos[:, None] * inv[None, :]
    cos, sin = jnp.cos(ang), jnp.sin(ang)
    return jnp.tile(cos, (1, 4)), jnp.tile(jnp.concatenate([-sin, sin], axis=1), (1, 2))


def kernel(x, meta_tokens, ffn1_norm, ffn1_w_gate, ffn1_w_up, ffn1_w_down, mix_norm, w_in, lambda_q1, lambda_k1, lambda_q2, lambda_k2, diff_norm, win_sink, win_norm, w_out, ffn2_norm, ffn2_w_gate, ffn2_w_up, ffn2_w_down, final_norm):
    batch, seq, d = x.shape
    assert ffn1_norm.shape[0] == 1, "single layer only"
    assert seq % ROW_TILE == 0 and seq % DIFF_TQ == 0 and seq % DIFF_TK == 0 and seq % WIN_QBLOCK == 0

    cos, sin_signed = _rope_tables(N_META + seq)
    ffn_in_weights = (ffn1_norm, ffn1_w_gate[0].astype(BF16), ffn1_w_up[0].astype(BF16),
                      ffn1_w_down[0].astype(BF16), mix_norm, w_in[0].astype(BF16))

    meta = _ffn_in(meta_tokens.astype(x.dtype), cos[:N_META], sin_signed[:N_META], *ffn_in_weights, N_META,
                   v_transposed=False)
    real = _ffn_in(x.reshape(batch * seq, d), cos[N_META:], sin_signed[N_META:], *ffn_in_weights, ROW_TILE,
                   v_transposed=True)
    _, _, ka_m, va_m, _, kb_m, vb_m = meta
    h1, qa, ka, vat, qb, kb, vbt = real

    lams = (lambda_q1, lambda_k1, lambda_q2, lambda_k2)
    out_a = _diff_attn(lams, diff_norm.reshape(2 * HEAD_DIM, 1), qa, ka_m, va_m.T, ka, vat, batch)

    out_b = _win_attn(win_sink.reshape(-1), qb, kb_m, vb_m.T, kb, vbt, batch)

    out = _out_ffn(h1, out_a, out_b, win_norm, w_out[0].astype(BF16), ffn2_norm,
                   ffn2_w_gate[0].astype(BF16), ffn2_w_up[0].astype(BF16), ffn2_w_down[0].astype(BF16),
                   final_norm.reshape(1, d))
    return out.reshape(batch, seq, d)
```

```python
import functools
import math

import jax
import jax.numpy as jnp
from jax import lax
from jax.experimental import pallas as pl
from jax.experimental.pallas import tpu as pltpu

N_META = 16
HEAD_DIM = 64
DIFF_HEADS = 4
WIN_KV_HEADS = 2
WIN_GROUP = 4
WINDOW = 128
ROPE_THETA = 10000.0
EPS = 1e-6
NEG = -1e30
LAMBDA_INIT = 0.8 - 0.6 * math.exp(-0.3 * 0)
QK_SCALE = HEAD_DIM ** -0.5
LOG2E = math.log2(math.e)

LANES = 128
BF16_SUBLANES = 16
VMEM_LIMIT_ROWWISE = 56 * 1024 * 1024
VMEM_LIMIT_ATTN = 40 * 1024 * 1024

ROW_TILE = 512
MXU_WIDTH = 256
FF_CHUNK = 6 * MXU_WIDTH
DIFF_TQ = 512
DIFF_TK = 1024
WIN_TQ = 128
WIN_SPAN = WIN_TQ + 2 * WINDOW
WIN_QBLOCK = 4096

F32 = jnp.float32
BF16 = jnp.bfloat16
_NT = (((1,), (1,)), ((), ()))


def _rms(x, gain):
    return x * lax.rsqrt(jnp.mean(x * x, axis=-1, keepdims=True) + EPS) * gain


def _rms_split(x, gain):
    return (x * gain).astype(BF16), lax.rsqrt(jnp.mean(x * x, axis=-1, keepdims=True) + EPS)


def _swiglu(xn, wg_ref, wu_ref, wd_ref, row_scale=None):
    d_ff = wg_ref.shape[1]
    acc = None
    for start in range(0, d_ff, FF_CHUNK):
        sl = slice(start, min(start + FF_CHUNK, d_ff))
        g = jnp.dot(xn, wg_ref[:, sl], preferred_element_type=F32)
        u = jnp.dot(xn, wu_ref[:, sl], preferred_element_type=F32)
        if row_scale is not None:
            g = g * row_scale
            u = u * row_scale
        a = (g * jax.nn.sigmoid(g) * u).astype(BF16)
        d = jnp.dot(a, wd_ref[sl, :], preferred_element_type=F32)
        acc = d if acc is None else acc + d
    return acc


def _rope_block(x, cos, sin_signed, first_half):
    partner = jnp.where(first_half, pltpu.roll(x, LANES - 32, 1), pltpu.roll(x, 32, 1))
    return x * cos + partner * sin_signed


def _ffn_in_kernel(x_ref, cos_ref, sin_ref, n1_ref, wg_ref, wu_ref, wd_ref, n2_ref, win_ref,
                   h_ref, qa_ref, ka_ref, va_ref, qb_ref, kb_ref, vb_ref, *, v_transposed):
    x = x_ref[...]
    xg, x_scale = _rms_split(x, n1_ref[...])
    h = x + 0.5 * _swiglu(xg, wg_ref, wu_ref, wd_ref, x_scale)
    h_ref[...] = h
    u, h_scale = _rms_split(h, n2_ref[...])

    rows = x.shape[0]
    cos = cos_ref[...]
    sin_signed = sin_ref[...]
    lane = lax.broadcasted_iota(jnp.int32, (rows, LANES), 1)
    first_half = (lane % HEAD_DIM) < (HEAD_DIM // 2)
    low = lane < HEAD_DIM

    def project(col0, width):
        return jnp.dot(u, win_ref[:, col0:col0 + width], preferred_element_type=F32) * h_scale

    def store_rope(dst_ref, z, scale):
        for i in range(z.shape[1] // LANES):
            blk = _rope_block(z[:, i * LANES:(i + 1) * LANES], cos, sin_signed, first_half)
            if scale != 1.0:
                blk = blk * scale
            dst_ref[:, i * LANES:(i + 1) * LANES] = blk.astype(dst_ref.dtype)

    def store_padded(dst_ref, z):
        swapped = pltpu.roll(z, HEAD_DIM, 1)
        zero = jnp.zeros_like(z)
        parts = (jnp.where(low, z, zero), jnp.where(low, zero, swapped),
                 jnp.where(low, swapped, zero), jnp.where(low, zero, z))
        for i, part in enumerate(parts):
            dst_ref[:, i * LANES:(i + 1) * LANES] = part.astype(dst_ref.dtype)

    def store_padded_transposed(dst_ref, z):
        zt = z.T
        swapped = jnp.concatenate([zt[HEAD_DIM:], zt[:HEAD_DIM]], axis=0)
        top = lax.broadcasted_iota(jnp.int32, zt.shape, 0) < HEAD_DIM
        zero = jnp.zeros_like(zt)
        parts = (jnp.where(top, zt, zero), jnp.where(top, zero, swapped),
                 jnp.where(top, swapped, zero), jnp.where(top, zero, zt))
        for i, part in enumerate(parts):
            dst_ref[i * LANES:(i + 1) * LANES, :] = part.astype(dst_ref.dtype)

    store_rope(qa_ref, project(0, 512), QK_SCALE * LOG2E)
    store_rope(ka_ref, project(512, 512), 1.0)
    store_rope(qb_ref, project(1536, 512), QK_SCALE * LOG2E)
    zkv = project(2048, 2 * LANES)
    store_padded(kb_ref, _rope_block(zkv[:, :LANES], cos, sin_signed, first_half))
    zva = project(1024, 512)
    zvb = zkv[:, LANES:]
    if v_transposed:
        for i in range(zva.shape[1] // LANES):
            va_ref[i * LANES:(i + 1) * LANES, :] = zva[:, i * LANES:(i + 1) * LANES].T.astype(va_ref.dtype)
        store_padded_transposed(vb_ref, zvb)
    else:
        va_ref[...] = zva.astype(va_ref.dtype)
        store_padded(vb_ref, zvb)


def _const_spec(shape):
    zeros = (0,) * len(shape)
    return pl.BlockSpec(shape, lambda *_: zeros, pipeline_mode=pl.Buffered(1))


def _ffn_in(x, cos, sin_signed, n1, wg, wu, wd, n2, win, row_tile, v_transposed):
    rows, d = x.shape
    pos_tiles = cos.shape[0] // row_tile
    row = lambda i: (i, 0)
    out_w = 512
    wide = pl.BlockSpec((row_tile, out_w), row)
    wide_shape = jax.ShapeDtypeStruct((rows, out_w), BF16)
    v_spec = pl.BlockSpec((out_w, row_tile), lambda i: (0, i)) if v_transposed else wide
    v_shape = jax.ShapeDtypeStruct((out_w, rows), BF16) if v_transposed else wide_shape
    return pl.pallas_call(
        functools.partial(_ffn_in_kernel, v_transposed=v_transposed),
        grid=(rows // row_tile,),
        in_specs=[
            pl.BlockSpec((row_tile, d), row),
            pl.BlockSpec((row_tile, LANES), lambda i: (i % pos_tiles, 0)),
            pl.BlockSpec((row_tile, LANES), lambda i: (i % pos_tiles, 0)),
            _const_spec(n1.shape), _const_spec(wg.shape), _const_spec(wu.shape), _const_spec(wd.shape),
            _const_spec(n2.shape), _const_spec(win.shape),
        ],
        out_specs=[pl.BlockSpec((row_tile, d), row), wide, wide, v_spec, wide, wide, v_spec],
        out_shape=[jax.ShapeDtypeStruct((rows, d), F32), wide_shape, wide_shape, v_shape, wide_shape, wide_shape,
                   v_shape],
        compiler_params=pltpu.CompilerParams(dimension_semantics=("parallel",),
                                             vmem_limit_bytes=VMEM_LIMIT_ROWWISE),
        name="ffn_in",
    )(x, cos, sin_signed, n1, wg, wu, wd, n2, win)


def _diff_attn_kernel(lq1_ref, lk1_ref, lq2_ref, lk2_ref, gain_ref, q_ref, km_ref, vmt_ref, k_ref, vt_ref,
                      o_ref, qz_sc, sa_sc, sb_sc, sm_sc, m_sc, acc_sc):
    lam = (jnp.exp(jnp.sum(lq1_ref[...] * lk1_ref[...], keepdims=True))
           - jnp.exp(jnp.sum(lq2_ref[...] * lk2_ref[...], keepdims=True)) + LAMBDA_INIT)
    tq, tk = DIFF_TQ, DIFF_TK
    vdim = vt_ref.shape[0]
    nq = q_ref.shape[0] // tq
    nk = vt_ref.shape[1] // tk

    def accumulate(c, cols, alpha, p, vt):
        acc_sc[c, :vdim, cols] = (alpha[:, cols] * acc_sc[c, :vdim, cols]
                                  + jnp.dot(vt, p.astype(BF16), preferred_element_type=F32))
        acc_sc[c, vdim:vdim + 1, cols] = (alpha[:, cols] * acc_sc[c, vdim:vdim + 1, cols]
                                          + jnp.sum(p, axis=0, keepdims=True))

    def load_q(i, slot):
        q = q_ref[pl.ds(pl.multiple_of(i * tq, tq), tq), :]
        lane = lax.broadcasted_iota(jnp.int32, q.shape, 1)
        zero = jnp.zeros_like(q)
        qz_sc[slot, 0] = jnp.where(lane < HEAD_DIM, q, zero)
        qz_sc[slot, 1] = jnp.where(lane < HEAD_DIM, zero, q)

    def produce(t, s_sc, slot):
        kt = k_ref[pl.ds(pl.multiple_of(t * tk, tk), tk), :]
        maxima = []
        for c in range(2):
            s = lax.dot_general(kt, qz_sc[slot, c], _NT, preferred_element_type=F32)
            s_sc[c] = s
            maxima.append(jnp.max(s, axis=0, keepdims=True))
        return tuple(maxima)

    def fold(c, s_ref, s_max, vt):
        m_prev = m_sc[c]
        m_new = jnp.maximum(m_prev, s_max)
        alpha = jnp.exp2(m_prev - m_new)
        m_sc[c] = m_new
        for n0 in range(0, tq, MXU_WIDTH):
            cols = slice(n0, n0 + MXU_WIDTH)
            accumulate(c, cols, alpha, jnp.exp2(s_ref[:, cols] - m_new[:, cols]), vt)

    def step(t, s_sc, maxima, t_next, s_next_sc, slot_next):
        vt = vt_ref[:, pl.ds(pl.multiple_of(t * tk, tk), tk)]
        stats = []
        for c in range(2):
            m_prev = m_sc[c]
            m_new = jnp.maximum(m_prev, maxima[c])
            stats.append((m_new, jnp.exp2(m_prev - m_new)))
            m_sc[c] = m_new
        kt_next = k_ref[pl.ds(pl.multiple_of(t_next * tk, tk), tk), :]
        next_max = [[], []]
        for c in range(2):
            m_new, alpha = stats[c]
            for n0 in range(0, tq, MXU_WIDTH):
                cols = slice(n0, n0 + MXU_WIDTH)
                s = lax.dot_general(kt_next, qz_sc[slot_next, c, cols, :], _NT, preferred_element_type=F32)
                s_next_sc[c, :, cols] = s
                next_max[c].append(jnp.max(s, axis=0, keepdims=True))
                accumulate(c, cols, alpha, jnp.exp2(s_sc[c, :, cols] - m_new[:, cols]), vt)
        return tuple(jnp.concatenate(parts, axis=1) for parts in next_max)

    def meta_scores(slot):
        for c in range(2):
            sm_sc[c] = lax.dot_general(km_ref[...], qz_sc[slot, c], _NT, preferred_element_type=F32)

    def consume_meta():
        vmt = vmt_ref[...]
        for c in range(2):
            fold(c, sm_sc.at[c], jnp.max(sm_sc[c], axis=0, keepdims=True), vmt)

    def finalize(i):
        def normalised(c):
            acc = acc_sc[c]
            return acc[:vdim] * (1.0 / acc[vdim:vdim + 1])

        a = normalised(0) - lam * normalised(1)
        ms = jnp.mean(a * a, axis=0, keepdims=True)
        y = a * lax.rsqrt(ms + EPS) * gain_ref[...] * (1.0 - LAMBDA_INIT)
        o_ref[pl.ds(pl.multiple_of(i * tq, tq), tq), :] = y.T.astype(o_ref.dtype)

    def query_tile(i, maxima_a):
        slot = i % 2
        for c in range(2):
            m_sc[c] = jnp.full(m_sc.shape[1:], NEG, F32)
            acc_sc[c] = jnp.zeros(acc_sc.shape[1:], F32)

        def pair(j, maxima_a):
            t = 2 * j
            maxima_b = step(t, sa_sc, maxima_a, t + 1, sb_sc, slot)
            return step(t + 1, sb_sc, maxima_b, t + 2, sa_sc, slot)

        maxima_a = lax.fori_loop(0, nk // 2 - 1, pair, maxima_a)
        meta_scores(slot)
        maxima_b = step(nk - 2, sa_sc, maxima_a, nk - 1, sb_sc, slot)
        load_q(jnp.minimum(i + 1, nq - 1), 1 - slot)
        maxima_next = step(nk - 1, sb_sc, maxima_b, 0, sa_sc, 1 - slot)
        consume_meta()
        finalize(i)
        return maxima_next

    load_q(0, 0)
    lax.fori_loop(0, nq, query_tile, produce(0, sa_sc, 0))


def _diff_attn(lams, gain_col, qa, ka_meta, vat_meta, ka, vat, batch):
    rows = qa.shape[0]
    seq = rows // batch
    vdim = 2 * HEAD_DIM
    tq, tk = DIFF_TQ, DIFF_TK
    assert (seq // tk) % 2 == 0
    lam_spec = pl.BlockSpec((1, HEAD_DIM), lambda b, h: (0, 0))
    return pl.pallas_call(
        _diff_attn_kernel,
        grid=(batch, DIFF_HEADS),
        in_specs=[
            lam_spec, lam_spec, lam_spec, lam_spec,
            pl.BlockSpec((vdim, 1), lambda b, h: (0, 0)),
            pl.BlockSpec((seq, LANES), lambda b, h: (b, h)),
            pl.BlockSpec((N_META, LANES), lambda b, h: (0, h)),
            pl.BlockSpec((vdim, N_META), lambda b, h: (h, 0)),
            pl.BlockSpec((seq, LANES), lambda b, h: (b, h)),
            pl.BlockSpec((vdim, seq), lambda b, h: (h, b)),
        ],
        out_specs=pl.BlockSpec((seq, LANES), lambda b, h: (b, h)),
        out_shape=jax.ShapeDtypeStruct((rows, DIFF_HEADS * vdim), BF16),
        scratch_shapes=[pltpu.VMEM((2, 2, tq, LANES), BF16),
                        pltpu.VMEM((2, tk, tq), F32), pltpu.VMEM((2, tk, tq), F32),
                        pltpu.VMEM((2, N_META, tq), F32),
                        pltpu.VMEM((2, 1, tq), F32),
                        pltpu.VMEM((2, vdim + BF16_SUBLANES, tq), F32)],
        compiler_params=pltpu.CompilerParams(dimension_semantics=("parallel", "parallel"),
                                             vmem_limit_bytes=VMEM_LIMIT_ATTN),
        name="diff_attn",
    )(*lams, gain_col, qa, ka_meta, vat_meta, ka, vat)


def _win_attn_kernel(sink_ref, q_ref, km_ref, vmt_ref, k_ref, vt_ref, o_ref,
                     delta_sc, sa_sc, sb_sc, sma_sc, smb_sc):
    g = pl.program_id(1)
    blk = pl.program_id(2)
    seq = k_ref.shape[0]
    tq, span = WIN_TQ, WIN_SPAN
    n = 2 * tq
    nt = q_ref.shape[0] // tq
    delta_sc[...] = (lax.broadcasted_iota(jnp.int32, (span, tq), 0)
                     - lax.broadcasted_iota(jnp.int32, (span, tq), 1))
    first_pair = lax.broadcasted_iota(jnp.int32, (1, n), 1) < tq
    sinks = [jnp.where(first_pair, sink_ref[g * WIN_GROUP + par], sink_ref[g * WIN_GROUP + 2 + par]) * LOG2E
             for par in range(2)]

    ones_row = (HEAD_DIM, 0)
    top_half = lax.broadcasted_iota(jnp.int32, (LANES, n), 0) < HEAD_DIM

    def key_start(i):
        t0 = blk * (nt * tq) + i * tq
        return t0, pl.multiple_of(jnp.clip(t0 - WINDOW, 0, seq - span), WINDOW)

    def produce_pieces(i, s_sc, sm_sc):
        t0, ks = key_start(i)
        off = ks - t0
        rows = pl.ds(pl.multiple_of(i * tq, tq), tq)

        def piece(par):
            delta = delta_sc[...]
            valid = (delta >= -WINDOW - off) & (delta <= WINDOW - off)
            q_cat = jnp.concatenate([q_ref[rows, :LANES], q_ref[rows, LANES:]], axis=0)
            part = slice(par * LANES, (par + 1) * LANES)
            k_cat = jnp.concatenate([k_ref[pl.ds(ks, span), part], km_ref[:, part]], axis=0)
            s_all = lax.dot_general(k_cat, q_cat, _NT, preferred_element_type=F32)
            s = s_all[:span]
            s = jnp.concatenate([jnp.where(valid, s[:, :tq], NEG), jnp.where(valid, s[:, tq:], NEG)], axis=1)
            sm = s_all[span:]
            s_sc[par] = s
            sm_sc[par] = sm
            return jnp.maximum(jnp.maximum(jnp.max(s, axis=0, keepdims=True),
                                           jnp.max(sm, axis=0, keepdims=True)), sinks[par])

        return [functools.partial(piece, par) for par in range(2)]

    def consume_pieces(i, s_sc, sm_sc, maxima):
        _, ks = key_start(i)

        def with_ones(vt, par):
            r = ones_row[par]
            ones = jnp.ones((BF16_SUBLANES, vt.shape[1]), BF16)
            return jnp.concatenate(([vt[:r]] if r else []) + [ones, vt[r + BF16_SUBLANES:]], axis=0)

        def piece(par):
            part = slice(par * LANES, (par + 1) * LANES)
            m = maxima[par]
            p = jnp.exp2(s_sc[par] - m)
            pm = jnp.exp2(sm_sc[par] - m)
            o = (jnp.dot(with_ones(vt_ref[part, pl.ds(ks, span)], par), p.astype(BF16),
                         preferred_element_type=F32)
                 + jnp.dot(with_ones(vmt_ref[part, :], par), pm.astype(BF16), preferred_element_type=F32))
            l = o[ones_row[par]:ones_row[par] + 1] + jnp.exp2(sinks[par] - m)
            return o * (1.0 / l)

        def store(outs):
            o_t = jnp.where(top_half, outs[0], outs[1])
            rows = pl.ds(pl.multiple_of(i * tq, tq), tq)
            for pair in range(WIN_GROUP // 2):
                o_ref[rows, pair * LANES:(pair + 1) * LANES] = (
                    o_t[:, pair * tq:(pair + 1) * tq].T.astype(o_ref.dtype))

        return [functools.partial(piece, par) for par in range(2)], store

    def produce(i, s_sc, sm_sc):
        return tuple(piece() for piece in produce_pieces(i, s_sc, sm_sc))

    def consume(i, s_sc, sm_sc, maxima):
        pieces, store = consume_pieces(i, s_sc, sm_sc, maxima)
        store([piece() for piece in pieces])

    def step(i, s_sc, sm_sc, maxima, i_next, s_next_sc, sm_next_sc):
        makers = produce_pieces(i_next, s_next_sc, sm_next_sc)
        users, store = consume_pieces(i, s_sc, sm_sc, maxima)
        next_maxima, outs = [], []
        for make, use in zip(makers, users):
            next_maxima.append(make())
            outs.append(use())
        store(outs)
        return tuple(next_maxima)

    def pair_of_tiles(j, maxima_a):
        i = 2 * j
        maxima_b = step(i, sa_sc, sma_sc, maxima_a, i + 1, sb_sc, smb_sc)
        return step(i + 1, sb_sc, smb_sc, maxima_b, i + 2, sa_sc, sma_sc)

    maxima_a = lax.fori_loop(0, nt // 2 - 1, pair_of_tiles, produce(0, sa_sc, sma_sc), unroll=5)
    maxima_b = step(nt - 2, sa_sc, sma_sc, maxima_a, nt - 1, sb_sc, smb_sc)
    consume(nt - 1, sb_sc, smb_sc, maxima_b)


def _win_attn(sink, qb, kb_meta, vbt_meta, kb, vbt, batch):
    rows = qb.shape[0]
    seq = rows // batch
    nblk = seq // WIN_QBLOCK
    assert (WIN_QBLOCK // WIN_TQ) % 2 == 0
    width = 2 * LANES
    n = 2 * WIN_TQ
    return pl.pallas_call(
        _win_attn_kernel,
        grid=(batch, WIN_KV_HEADS, nblk),
        in_specs=[
            pl.BlockSpec(memory_space=pltpu.SMEM),
            pl.BlockSpec((WIN_QBLOCK, width), lambda b, g, i: (b * nblk + i, g)),
            pl.BlockSpec((N_META, width), lambda b, g, i: (0, g)),
            pl.BlockSpec((width, N_META), lambda b, g, i: (g, 0)),
            pl.BlockSpec((seq, width), lambda b, g, i: (b, g)),
            pl.BlockSpec((width, seq), lambda b, g, i: (g, b)),
        ],
        out_specs=pl.BlockSpec((WIN_QBLOCK, width), lambda b, g, i: (b * nblk + i, g)),
        out_shape=jax.ShapeDtypeStruct((rows, WIN_KV_HEADS * width), BF16),
        scratch_shapes=[pltpu.VMEM((WIN_SPAN, WIN_TQ), jnp.int32),
                        pltpu.VMEM((2, WIN_SPAN, n), F32), pltpu.VMEM((2, WIN_SPAN, n), F32),
                        pltpu.VMEM((2, N_META, n), F32), pltpu.VMEM((2, N_META, n), F32)],
        compiler_params=pltpu.CompilerParams(dimension_semantics=("parallel", "parallel", "arbitrary"),
                                             vmem_limit_bytes=VMEM_LIMIT_ATTN),
        name="win_attn",
    )(sink, qb, kb_meta, vbt_meta, kb, vbt)


def _out_ffn_kernel(h_ref, oa_ref, ob_ref, wn_ref, wo_ref, n_ref, wg_ref, wu_ref, wd_ref, fn_ref, out_ref):
    half = oa_ref.shape[1]
    ob, ob_scale = _rms_split(ob_ref[...].astype(F32), wn_ref[...])
    mix = (jnp.dot(oa_ref[...], wo_ref[:half, :], preferred_element_type=F32)
           + jnp.dot(ob, wo_ref[half:, :], preferred_element_type=F32) * ob_scale)
    h = h_ref[...] + mix
    hg, h_scale = _rms_split(h, n_ref[...])
    h = h + 0.5 * _swiglu(hg, wg_ref, wu_ref, wd_ref, h_scale)
    out_ref[...] = _rms(h, fn_ref[...])


def _out_ffn(h1, oa, ob, wn, wo, n, wg, wu, wd, fn):
    rows, d = h1.shape
    half = oa.shape[1]
    row = lambda i: (i, 0)
    return pl.pallas_call(
        _out_ffn_kernel,
        grid=(rows // ROW_TILE,),
        in_specs=[
            pl.BlockSpec((ROW_TILE, d), row), pl.BlockSpec((ROW_TILE, half), row), pl.BlockSpec((ROW_TILE, half), row),
            _const_spec(wn.shape), _const_spec(wo.shape), _const_spec(n.shape),
            _const_spec(wg.shape), _const_spec(wu.shape), _const_spec(wd.shape), _const_spec(fn.shape),
        ],
        out_specs=pl.BlockSpec((ROW_TILE, d), row),
        out_shape=jax.ShapeDtypeStruct((rows, d), F32),
        compiler_params=pltpu.CompilerParams(dimension_semantics=("parallel",),
                                             vmem_limit_bytes=VMEM_LIMIT_ROWWISE),
        name="out_ffn",
    )(h1, oa, ob, wn, wo, n, wg, wu, wd, fn)


def _rope_tables(length):
    pos = jnp.arange(length, dtype=F32)
    inv = ROPE_THETA ** (-jnp.arange(0, HEAD_DIM, 2, dtype=F32) / HEAD_DIM)
    ang = pos[:, None] * inv[None, :]
    cos, sin = jnp.cos(ang), jnp.sin(ang)
    return jnp.tile(cos, (1, 4)), jnp.tile(jnp.concatenate([-sin, sin], axis=1), (1, 2))


def kernel(x, meta_tokens, ffn1_norm, ffn1_w_gate, ffn1_w_up, ffn1_w_down, mix_norm, w_in, lambda_q1, lambda_k1, lambda_q2, lambda_k2, diff_norm, win_sink, win_norm, w_out, ffn2_norm, ffn2_w_gate, ffn2_w_up, ffn2_w_down, final_norm):
    batch, seq, d = x.shape
    assert ffn1_norm.shape[0] == 1, "single layer only"
    assert seq % ROW_TILE == 0 and seq % DIFF_TQ == 0 and seq % DIFF_TK == 0 and seq % WIN_QBLOCK == 0

    cos, sin_signed = _rope_tables(N_META + seq)
    ffn_in_weights = (ffn1_norm, ffn1_w_gate[0].astype(BF16), ffn1_w_up[0].astype(BF16),
                      ffn1_w_down[0].astype(BF16), mix_norm, w_in[0].astype(BF16))

    meta = _ffn_in(meta_tokens.astype(x.dtype), cos[:N_META], sin_signed[:N_META], *ffn_in_weights, N_META,
                   v_transposed=False)
    real = _ffn_in(x.reshape(batch * seq, d), cos[N_META:], sin_signed[N_META:], *ffn_in_weights, ROW_TILE,
                   v_transposed=True)
    _, _, ka_m, va_m, _, kb_m, vb_m = meta
    h1, qa, ka, vat, qb, kb, vbt = real

    lams = (lambda_q1, lambda_k1, lambda_q2, lambda_k2)
    out_a = _diff_attn(lams, diff_norm.reshape(2 * HEAD_DIM, 1), qa, ka_m, va_m.T, ka, vat, batch)

    out_b = _win_attn(win_sink.reshape(-1), qb, kb_m, vb_m.T, kb, vbt, batch)

    out = _out_ffn(h1, out_a, out_b, win_norm, w_out[0].astype(BF16), ffn2_norm,
                   ffn2_w_gate[0].astype(BF16), ffn2_w_up[0].astype(BF16), ffn2_w_down[0].astype(BF16),
                   final_norm.reshape(1, d))
    return out.reshape(batch, seq, d)
```

```python
import functools
import math

import jax
import jax.numpy as jnp
from jax import lax
from jax.experimental import pallas as pl
from jax.experimental.pallas import tpu as pltpu

N_META = 16
HEAD_DIM = 64
DIFF_HEADS = 4
WIN_KV_HEADS = 2
WIN_GROUP = 4
WINDOW = 128
ROPE_THETA = 10000.0
EPS = 1e-6
NEG = -1e30
LAMBDA_INIT = 0.8 - 0.6 * math.exp(-0.3 * 0)
QK_SCALE = HEAD_DIM ** -0.5
LOG2E = math.log2(math.e)

LANES = 128
BF16_SUBLANES = 16
VMEM_LIMIT_ROWWISE = 56 * 1024 * 1024
VMEM_LIMIT_ATTN = 40 * 1024 * 1024

ROW_TILE = 512
MXU_WIDTH = 256
FF_CHUNK = 6 * MXU_WIDTH
DIFF_TQ = 1024
DIFF_TK = 1024
WIN_TQ = 128
WIN_SPAN = WIN_TQ + 2 * WINDOW
WIN_QBLOCK = 4096

F32 = jnp.float32
BF16 = jnp.bfloat16
_NT = (((1,), (1,)), ((), ()))


def _rms(x, gain):
    return x * lax.rsqrt(jnp.mean(x * x, axis=-1, keepdims=True) + EPS) * gain


def _rms_split(x, gain):
    return (x * gain).astype(BF16), lax.rsqrt(jnp.mean(x * x, axis=-1, keepdims=True) + EPS)


def _swiglu(xn, wg_ref, wu_ref, wd_ref, row_scale=None):
    d_ff = wg_ref.shape[1]
    acc = None
    for start in range(0, d_ff, FF_CHUNK):
        sl = slice(start, min(start + FF_CHUNK, d_ff))
        g = jnp.dot(xn, wg_ref[:, sl], preferred_element_type=F32)
        u = jnp.dot(xn, wu_ref[:, sl], preferred_element_type=F32)
        if row_scale is not None:
            g = g * row_scale
            u = u * row_scale
        a = (g * jax.nn.sigmoid(g) * u).astype(BF16)
        d = jnp.dot(a, wd_ref[sl, :], preferred_element_type=F32)
        acc = d if acc is None else acc + d
    return acc


def _rope_block(x, cos, sin_signed, first_half):
    partner = jnp.where(first_half, pltpu.roll(x, LANES - 32, 1), pltpu.roll(x, 32, 1))
    return x * cos + partner * sin_signed


def _ffn_in_kernel(x_ref, cos_ref, sin_ref, n1_ref, wg_ref, wu_ref, wd_ref, n2_ref, win_ref,
                   h_ref, qa_ref, ka_ref, va_ref, qb_ref, kb_ref, vb_ref, *, v_transposed):
    x = x_ref[...]
    xg, x_scale = _rms_split(x, n1_ref[...])
    h = x + 0.5 * _swiglu(xg, wg_ref, wu_ref, wd_ref, x_scale)
    h_ref[...] = h
    u, h_scale = _rms_split(h, n2_ref[...])

    rows = x.shape[0]
    cos = cos_ref[...]
    sin_signed = sin_ref[...]
    lane = lax.broadcasted_iota(jnp.int32, (rows, LANES), 1)
    first_half = (lane % HEAD_DIM) < (HEAD_DIM // 2)
    low = lane < HEAD_DIM

    def project(col0, width):
        return jnp.dot(u, win_ref[:, col0:col0 + width], preferred_element_type=F32) * h_scale

    def store_rope(dst_ref, z, scale):
        for i in range(z.shape[1] // LANES):
            blk = _rope_block(z[:, i * LANES:(i + 1) * LANES], cos, sin_signed, first_half)
            if scale != 1.0:
                blk = blk * scale
            dst_ref[:, i * LANES:(i + 1) * LANES] = blk.astype(dst_ref.dtype)

    def store_padded(dst_ref, z):
        swapped = pltpu.roll(z, HEAD_DIM, 1)
        zero = jnp.zeros_like(z)
        parts = (jnp.where(low, z, zero), jnp.where(low, zero, swapped),
                 jnp.where(low, swapped, zero), jnp.where(low, zero, z))
        for i, part in enumerate(parts):
            dst_ref[:, i * LANES:(i + 1) * LANES] = part.astype(dst_ref.dtype)

    def store_padded_transposed(dst_ref, z):
        zt = z.T
        swapped = jnp.concatenate([zt[HEAD_DIM:], zt[:HEAD_DIM]], axis=0)
        top = lax.broadcasted_iota(jnp.int32, zt.shape, 0) < HEAD_DIM
        zero = jnp.zeros_like(zt)
        parts = (jnp.where(top, zt, zero), jnp.where(top, zero, swapped),
                 jnp.where(top, swapped, zero), jnp.where(top, zero, zt))
        for i, part in enumerate(parts):
            dst_ref[i * LANES:(i + 1) * LANES, :] = part.astype(dst_ref.dtype)

    store_rope(qa_ref, project(0, 512), QK_SCALE * LOG2E)
    store_rope(ka_ref, project(512, 512), 1.0)
    store_rope(qb_ref, project(1536, 512), QK_SCALE * LOG2E)
    zkv = project(2048, 2 * LANES)
    store_padded(kb_ref, _rope_block(zkv[:, :LANES], cos, sin_signed, first_half))
    zva = project(1024, 512)
    zvb = zkv[:, LANES:]
    if v_transposed:
        for i in range(zva.shape[1] // LANES):
            va_ref[i * LANES:(i + 1) * LANES, :] = zva[:, i * LANES:(i + 1) * LANES].T.astype(va_ref.dtype)
        store_padded_transposed(vb_ref, zvb)
    else:
        va_ref[...] = zva.astype(va_ref.dtype)
        store_padded(vb_ref, zvb)


def _const_spec(shape):
    zeros = (0,) * len(shape)
    return pl.BlockSpec(shape, lambda *_: zeros, pipeline_mode=pl.Buffered(1))


def _ffn_in(x, cos, sin_signed, n1, wg, wu, wd, n2, win, row_tile, v_transposed):
    rows, d = x.shape
    pos_tiles = cos.shape[0] // row_tile
    row = lambda i: (i, 0)
    out_w = 512
    wide = pl.BlockSpec((row_tile, out_w), row)
    wide_shape = jax.ShapeDtypeStruct((rows, out_w), BF16)
    v_spec = pl.BlockSpec((out_w, row_tile), lambda i: (0, i)) if v_transposed else wide
    v_shape = jax.ShapeDtypeStruct((out_w, rows), BF16) if v_transposed else wide_shape
    return pl.pallas_call(
        functools.partial(_ffn_in_kernel, v_transposed=v_transposed),
        grid=(rows // row_tile,),
        in_specs=[
            pl.BlockSpec((row_tile, d), row),
            pl.BlockSpec((row_tile, LANES), lambda i: (i % pos_tiles, 0)),
            pl.BlockSpec((row_tile, LANES), lambda i: (i % pos_tiles, 0)),
            _const_spec(n1.shape), _const_spec(wg.shape), _const_spec(wu.shape), _const_spec(wd.shape),
            _const_spec(n2.shape), _const_spec(win.shape),
        ],
        out_specs=[pl.BlockSpec((row_tile, d), row), wide, wide, v_spec, wide, wide, v_spec],
        out_shape=[jax.ShapeDtypeStruct((rows, d), F32), wide_shape, wide_shape, v_shape, wide_shape, wide_shape,
                   v_shape],
        compiler_params=pltpu.CompilerParams(dimension_semantics=("parallel",),
                                             vmem_limit_bytes=VMEM_LIMIT_ROWWISE),
        name="ffn_in",
    )(x, cos, sin_signed, n1, wg, wu, wd, n2, win)


def _diff_attn_kernel(lq1_ref, lk1_ref, lq2_ref, lk2_ref, gain_ref, q_ref, km_ref, vmt_ref, k_ref, vt_ref,
                      o_ref, qz_sc, sa_sc, sb_sc, sm_sc, m_sc, acc_sc):
    lam = (jnp.exp(jnp.sum(lq1_ref[...] * lk1_ref[...], keepdims=True))
           - jnp.exp(jnp.sum(lq2_ref[...] * lk2_ref[...], keepdims=True)) + LAMBDA_INIT)
    tq, tk = DIFF_TQ, DIFF_TK
    vdim = vt_ref.shape[0]
    nq = q_ref.shape[0] // tq
    nk = vt_ref.shape[1] // tk

    def with_ones(vt):
        return jnp.concatenate([vt, jnp.ones((BF16_SUBLANES, vt.shape[1]), BF16)], axis=0)

    def load_q(i, slot):
        q = q_ref[pl.ds(pl.multiple_of(i * tq, tq), tq), :]
        lane = lax.broadcasted_iota(jnp.int32, q.shape, 1)
        zero = jnp.zeros_like(q)
        qz_sc[slot, 0] = jnp.where(lane < HEAD_DIM, q, zero)
        qz_sc[slot, 1] = jnp.where(lane < HEAD_DIM, zero, q)

    def produce(t, s_sc, slot):
        kt = k_ref[pl.ds(pl.multiple_of(t * tk, tk), tk), :]
        maxima = []
        for c in range(2):
            s = lax.dot_general(kt, qz_sc[slot, c], _NT, preferred_element_type=F32)
            s_sc[c] = s
            maxima.append(jnp.max(s, axis=0, keepdims=True))
        return tuple(maxima)

    def fold(c, s_ref, s_max, vt):
        m_prev = m_sc[c]
        m_new = jnp.maximum(m_prev, s_max)
        alpha = jnp.exp2(m_prev - m_new)
        m_sc[c] = m_new
        for n0 in range(0, tq, MXU_WIDTH):
            cols = slice(n0, n0 + MXU_WIDTH)
            p = jnp.exp2(s_ref[:, cols] - m_new[:, cols])
            acc_sc[c, :, cols] = (alpha[:, cols] * acc_sc[c, :, cols]
                                  + jnp.dot(vt, p.astype(BF16), preferred_element_type=F32))

    def step(t, s_sc, maxima, t_next, s_next_sc, slot_next):
        vt = with_ones(vt_ref[:, pl.ds(pl.multiple_of(t * tk, tk), tk)])
        stats = []
        for c in range(2):
            m_prev = m_sc[c]
            m_new = jnp.maximum(m_prev, maxima[c])
            stats.append((m_new, jnp.exp2(m_prev - m_new)))
            m_sc[c] = m_new
        kt_next = k_ref[pl.ds(pl.multiple_of(t_next * tk, tk), tk), :]
        next_max = [[], []]
        for c in range(2):
            m_new, alpha = stats[c]
            for n0 in range(0, tq, MXU_WIDTH):
                cols = slice(n0, n0 + MXU_WIDTH)
                s = lax.dot_general(kt_next, qz_sc[slot_next, c, cols, :], _NT, preferred_element_type=F32)
                s_next_sc[c, :, cols] = s
                next_max[c].append(jnp.max(s, axis=0, keepdims=True))
                p = jnp.exp2(s_sc[c, :, cols] - m_new[:, cols])
                acc_sc[c, :, cols] = (alpha[:, cols] * acc_sc[c, :, cols]
                                      + jnp.dot(vt, p.astype(BF16), preferred_element_type=F32))
        return tuple(jnp.concatenate(parts, axis=1) for parts in next_max)

    def meta_scores(slot):
        for c in range(2):
            sm_sc[c] = lax.dot_general(km_ref[...], qz_sc[slot, c], _NT, preferred_element_type=F32)

    def consume_meta():
        vmt = with_ones(vmt_ref[...])
        for c in range(2):
            fold(c, sm_sc.at[c], jnp.max(sm_sc[c], axis=0, keepdims=True), vmt)

    def finalize(i):
        def normalised(c):
            acc = acc_sc[c]
            return acc[:vdim] * (1.0 / acc[vdim:vdim + 1])

        a = normalised(0) - lam * normalised(1)
        ms = jnp.mean(a * a, axis=0, keepdims=True)
        y = a * lax.rsqrt(ms + EPS) * gain_ref[...] * (1.0 - LAMBDA_INIT)
        o_ref[pl.ds(pl.multiple_of(i * tq, tq), tq), :] = y.T.astype(o_ref.dtype)

    def query_tile(i, maxima_a):
        slot = i % 2
        for c in range(2):
            m_sc[c] = jnp.full(m_sc.shape[1:], NEG, F32)
            acc_sc[c] = jnp.zeros(acc_sc.shape[1:], F32)

        def pair(j, maxima_a):
            t = 2 * j
            maxima_b = step(t, sa_sc, maxima_a, t + 1, sb_sc, slot)
            return step(t + 1, sb_sc, maxima_b, t + 2, sa_sc, slot)

        maxima_a = lax.fori_loop(0, nk // 2 - 1, pair, maxima_a)
        meta_scores(slot)
        maxima_b = step(nk - 2, sa_sc, maxima_a, nk - 1, sb_sc, slot)
        load_q(jnp.minimum(i + 1, nq - 1), 1 - slot)
        maxima_next = step(nk - 1, sb_sc, maxima_b, 0, sa_sc, 1 - slot)
        consume_meta()
        finalize(i)
        return maxima_next

    load_q(0, 0)
    lax.fori_loop(0, nq, query_tile, produce(0, sa_sc, 0))


def _diff_attn(lams, gain_col, qa, ka_meta, vat_meta, ka, vat, batch):
    rows = qa.shape[0]
    seq = rows // batch
    vdim = 2 * HEAD_DIM
    tq, tk = DIFF_TQ, DIFF_TK
    assert (seq // tk) % 2 == 0
    lam_spec = pl.BlockSpec((1, HEAD_DIM), lambda b, h: (0, 0))
    return pl.pallas_call(
        _diff_attn_kernel,
        grid=(batch, DIFF_HEADS),
        in_specs=[
            lam_spec, lam_spec, lam_spec, lam_spec,
            pl.BlockSpec((vdim, 1), lambda b, h: (0, 0)),
            pl.BlockSpec((seq, LANES), lambda b, h: (b, h)),
            pl.BlockSpec((N_META, LANES), lambda b, h: (0, h)),
            pl.BlockSpec((vdim, N_META), lambda b, h: (h, 0)),
            pl.BlockSpec((seq, LANES), lambda b, h: (b, h)),
            pl.BlockSpec((vdim, seq), lambda b, h: (h, b)),
        ],
        out_specs=pl.BlockSpec((seq, LANES), lambda b, h: (b, h)),
        out_shape=jax.ShapeDtypeStruct((rows, DIFF_HEADS * vdim), BF16),
        scratch_shapes=[pltpu.VMEM((2, 2, tq, LANES), BF16),
                        pltpu.VMEM((2, tk, tq), F32), pltpu.VMEM((2, tk, tq), F32),
                        pltpu.VMEM((2, N_META, tq), F32),
                        pltpu.VMEM((2, 1, tq), F32),
                        pltpu.VMEM((2, vdim + BF16_SUBLANES, tq), F32)],
        compiler_params=pltpu.CompilerParams(dimension_semantics=("parallel", "parallel"),
                                             vmem_limit_bytes=VMEM_LIMIT_ATTN),
        name="diff_attn",
    )(*lams, gain_col, qa, ka_meta, vat_meta, ka, vat)


def _win_attn_kernel(sink_ref, q_ref, km_ref, vmt_ref, k_ref, vt_ref, o_ref,
                     delta_sc, sa_sc, sb_sc, sma_sc, smb_sc):
    g = pl.program_id(1)
    blk = pl.program_id(2)
    seq = k_ref.shape[0]
    tq, span = WIN_TQ, WIN_SPAN
    n = 2 * tq
    nt = q_ref.shape[0] // tq
    delta_sc[...] = (lax.broadcasted_iota(jnp.int32, (span, tq), 0)
                     - lax.broadcasted_iota(jnp.int32, (span, tq), 1))
    first_pair = lax.broadcasted_iota(jnp.int32, (1, n), 1) < tq
    sinks = [jnp.where(first_pair, sink_ref[g * WIN_GROUP + par], sink_ref[g * WIN_GROUP + 2 + par]) * LOG2E
             for par in range(2)]

    ones_row = (HEAD_DIM, 0)
    top_half = lax.broadcasted_iota(jnp.int32, (LANES, n), 0) < HEAD_DIM

    def key_start(i):
        t0 = blk * (nt * tq) + i * tq
        return t0, pl.multiple_of(jnp.clip(t0 - WINDOW, 0, seq - span), WINDOW)

    def produce_pieces(i, s_sc, sm_sc):
        t0, ks = key_start(i)
        off = ks - t0
        rows = pl.ds(pl.multiple_of(i * tq, tq), tq)

        def piece(par):
            delta = delta_sc[...]
            valid = (delta >= -WINDOW - off) & (delta <= WINDOW - off)
            q_cat = jnp.concatenate([q_ref[rows, :LANES], q_ref[rows, LANES:]], axis=0)
            part = slice(par * LANES, (par + 1) * LANES)
            k_cat = jnp.concatenate([k_ref[pl.ds(ks, span), part], km_ref[:, part]], axis=0)
            s_all = lax.dot_general(k_cat, q_cat, _NT, preferred_element_type=F32)
            s = s_all[:span]
            s = jnp.concatenate([jnp.where(valid, s[:, :tq], NEG), jnp.where(valid, s[:, tq:], NEG)], axis=1)
            sm = s_all[span:]
            s_sc[par] = s
            sm_sc[par] = sm
            return jnp.maximum(jnp.maximum(jnp.max(s, axis=0, keepdims=True),
                                           jnp.max(sm, axis=0, keepdims=True)), sinks[par])

        return [functools.partial(piece, par) for par in range(2)]

    def consume_pieces(i, s_sc, sm_sc, maxima):
        _, ks = key_start(i)

        def with_ones(vt, par):
            r = ones_row[par]
            ones = jnp.ones((BF16_SUBLANES, vt.shape[1]), BF16)
            return jnp.concatenate(([vt[:r]] if r else []) + [ones, vt[r + BF16_SUBLANES:]], axis=0)

        def piece(par):
            part = slice(par * LANES, (par + 1) * LANES)
            m = maxima[par]
            p = jnp.exp2(s_sc[par] - m)
            pm = jnp.exp2(sm_sc[par] - m)
            o = (jnp.dot(with_ones(vt_ref[part, pl.ds(ks, span)], par), p.astype(BF16),
                         preferred_element_type=F32)
                 + jnp.dot(with_ones(vmt_ref[part, :], par), pm.astype(BF16), preferred_element_type=F32))
            l = o[ones_row[par]:ones_row[par] + 1] + jnp.exp2(sinks[par] - m)
            return o * (1.0 / l)

        def store(outs):
            o_t = jnp.where(top_half, outs[0], outs[1])
            rows = pl.ds(pl.multiple_of(i * tq, tq), tq)
            for pair in range(WIN_GROUP // 2):
                o_ref[rows, pair * LANES:(pair + 1) * LANES] = (
                    o_t[:, pair * tq:(pair + 1) * tq].T.astype(o_ref.dtype))

        return [functools.partial(piece, par) for par in range(2)], store

    def produce(i, s_sc, sm_sc):
        return tuple(piece() for piece in produce_pieces(i, s_sc, sm_sc))

    def consume(i, s_sc, sm_sc, maxima):
        pieces, store = consume_pieces(i, s_sc, sm_sc, maxima)
        store([piece() for piece in pieces])

    def step(i, s_sc, sm_sc, maxima, i_next, s_next_sc, sm_next_sc):
        makers = produce_pieces(i_next, s_next_sc, sm_next_sc)
        users, store = consume_pieces(i, s_sc, sm_sc, maxima)
        next_maxima, outs = [], []
        for make, use in zip(makers, users):
            next_maxima.append(make())
            outs.append(use())
        store(outs)
        return tuple(next_maxima)

    def pair_of_tiles(j, maxima_a):
        i = 2 * j
        maxima_b = step(i, sa_sc, sma_sc, maxima_a, i + 1, sb_sc, smb_sc)
        return step(i + 1, sb_sc, smb_sc, maxima_b, i + 2, sa_sc, sma_sc)

    maxima_a = lax.fori_loop(0, nt // 2 - 1, pair_of_tiles, produce(0, sa_sc, sma_sc), unroll=5)
    maxima_b = step(nt - 2, sa_sc, sma_sc, maxima_a, nt - 1, sb_sc, smb_sc)
    consume(nt - 1, sb_sc, smb_sc, maxima_b)


def _win_attn(sink, qb, kb_meta, vbt_meta, kb, vbt, batch):
    rows = qb.shape[0]
    seq = rows // batch
    nblk = seq // WIN_QBLOCK
    assert (WIN_QBLOCK // WIN_TQ) % 2 == 0
    width = 2 * LANES
    n = 2 * WIN_TQ
    return pl.pallas_call(
        _win_attn_kernel,
        grid=(batch, WIN_KV_HEADS, nblk),
        in_specs=[
            pl.BlockSpec(memory_space=pltpu.SMEM),
            pl.BlockSpec((WIN_QBLOCK, width), lambda b, g, i: (b * nblk + i, g)),
            pl.BlockSpec((N_META, width), lambda b, g, i: (0, g)),
            pl.BlockSpec((width, N_META), lambda b, g, i: (g, 0)),
            pl.BlockSpec((seq, width), lambda b, g, i: (b, g)),
            pl.BlockSpec((width, seq), lambda b, g, i: (g, b)),
        ],
        out_specs=pl.BlockSpec((WIN_QBLOCK, width), lambda b, g, i: (b * nblk + i, g)),
        out_shape=jax.ShapeDtypeStruct((rows, WIN_KV_HEADS * width), BF16),
        scratch_shapes=[pltpu.VMEM((WIN_SPAN, WIN_TQ), jnp.int32),
                        pltpu.VMEM((2, WIN_SPAN, n), F32), pltpu.VMEM((2, WIN_SPAN, n), F32),
                        pltpu.VMEM((2, N_META, n), F32), pltpu.VMEM((2, N_META, n), F32)],
        compiler_params=pltpu.CompilerParams(dimension_semantics=("parallel", "parallel", "arbitrary"),
                                             vmem_limit_bytes=VMEM_LIMIT_ATTN),
        name="win_attn",
    )(sink, qb, kb_meta, vbt_meta, kb, vbt)


def _out_ffn_kernel(h_ref, oa_ref, ob_ref, wn_ref, wo_ref, n_ref, wg_ref, wu_ref, wd_ref, fn_ref, out_ref):
    half = oa_ref.shape[1]
    ob, ob_scale = _rms_split(ob_ref[...].astype(F32), wn_ref[...])
    mix = (jnp.dot(oa_ref[...], wo_ref[:half, :], preferred_element_type=F32)
           + jnp.dot(ob, wo_ref[half:, :], preferred_element_type=F32) * ob_scale)
    h = h_ref[...] + mix
    hg, h_scale = _rms_split(h, n_ref[...])
    h = h + 0.5 * _swiglu(hg, wg_ref, wu_ref, wd_ref, h_scale)
    out_ref[...] = _rms(h, fn_ref[...])


def _out_ffn(h1, oa, ob, wn, wo, n, wg, wu, wd, fn):
    rows, d = h1.shape
    half = oa.shape[1]
    row = lambda i: (i, 0)
    return pl.pallas_call(
        _out_ffn_kernel,
        grid=(rows // ROW_TILE,),
        in_specs=[
            pl.BlockSpec((ROW_TILE, d), row), pl.BlockSpec((ROW_TILE, half), row), pl.BlockSpec((ROW_TILE, half), row),
            _const_spec(wn.shape), _const_spec(wo.shape), _const_spec(n.shape),
            _const_spec(wg.shape), _const_spec(wu.shape), _const_spec(wd.shape), _const_spec(fn.shape),
        ],
        out_specs=pl.BlockSpec((ROW_TILE, d), row),
        out_shape=jax.ShapeDtypeStruct((rows, d), F32),
        compiler_params=pltpu.CompilerParams(dimension_semantics=("parallel",),
                                             vmem_limit_bytes=VMEM_LIMIT_ROWWISE),
        name="out_ffn",
    )(h1, oa, ob, wn, wo, n, wg, wu, wd, fn)


def _rope_tables(length):
    pos = jnp.arange(length, dtype=F32)
    inv = ROPE_THETA ** (-jnp.arange(0, HEAD_DIM, 2, dtype=F32) / HEAD_DIM)
    ang = pos[:, None] * inv[None, :]
    cos, sin = jnp.cos(ang), jnp.sin(ang)
    return jnp.tile(cos, (1, 4)), jnp.tile(jnp.concatenate([-sin, sin], axis=1), (1, 2))


def kernel(x, meta_tokens, ffn1_norm, ffn1_w_gate, ffn1_w_up, ffn1_w_down, mix_norm, w_in, lambda_q1, lambda_k1, lambda_q2, lambda_k2, diff_norm, win_sink, win_norm, w_out, ffn2_norm, ffn2_w_gate, ffn2_w_up, ffn2_w_down, final_norm):
    batch, seq, d = x.shape
    assert ffn1_norm.shape[0] == 1, "single layer only"
    assert seq % ROW_TILE == 0 and seq % DIFF_TQ == 0 and seq % DIFF_TK == 0 and seq % WIN_QBLOCK == 0

    cos, sin_signed = _rope_tables(N_META + seq)
    ffn_in_weights = (ffn1_norm, ffn1_w_gate[0].astype(BF16), ffn1_w_up[0].astype(BF16),
                      ffn1_w_down[0].astype(BF16), mix_norm, w_in[0].astype(BF16))

    meta = _ffn_in(meta_tokens.astype(x.dtype), cos[:N_META], sin_signed[:N_META], *ffn_in_weights, N_META,
                   v_transposed=False)
    real = _ffn_in(x.reshape(batch * seq, d), cos[N_META:], sin_signed[N_META:], *ffn_in_weights, ROW_TILE,
                   v_transposed=True)
    _, _, ka_m, va_m, _, kb_m, vb_m = meta
    h1, qa, ka, vat, qb, kb, vbt = real

    lams = (lambda_q1, lambda_k1, lambda_q2, lambda_k2)
    out_a = _diff_attn(lams, diff_norm.reshape(2 * HEAD_DIM, 1), qa, ka_m, va_m.T, ka, vat, batch)

    out_b = _win_attn(win_sink.reshape(-1), qb, kb_m, vb_m.T, kb, vbt, batch)

    out = _out_ffn(h1, out_a, out_b, win_norm, w_out[0].astype(BF16), ffn2_norm,
                   ffn2_w_gate[0].astype(BF16), ffn2_w_up[0].astype(BF16), ffn2_w_down[0].astype(BF16),
                   final_norm.reshape(1, d))
    return out.reshape(batch, seq, d)
```

```python
import functools
import math

import jax
import jax.numpy as jnp
from jax import lax
from jax.experimental import pallas as pl
from jax.experimental.pallas import tpu as pltpu

N_META = 16
HEAD_DIM = 64
DIFF_HEADS = 4
WIN_KV_HEADS = 2
WIN_GROUP = 4
WINDOW = 128
ROPE_THETA = 10000.0
EPS = 1e-6
NEG = -1e30
LAMBDA_INIT = 0.8 - 0.6 * math.exp(-0.3 * 0)
QK_SCALE = HEAD_DIM ** -0.5
LOG2E = math.log2(math.e)

LANES = 128
BF16_SUBLANES = 16
VMEM_LIMIT_ROWWISE = 56 * 1024 * 1024
VMEM_LIMIT_ATTN = 40 * 1024 * 1024

ROW_TILE = 512
MXU_WIDTH = 256
FF_CHUNK = 6 * MXU_WIDTH
DIFF_TQ = 2048
DIFF_TK = 512
WIN_TQ = 128
WIN_SPAN = WIN_TQ + 2 * WINDOW
WIN_QBLOCK = 4096

F32 = jnp.float32
BF16 = jnp.bfloat16
_NT = (((1,), (1,)), ((), ()))


def _rms(x, gain):
    return x * lax.rsqrt(jnp.mean(x * x, axis=-1, keepdims=True) + EPS) * gain


def _rms_split(x, gain):
    return (x * gain).astype(BF16), lax.rsqrt(jnp.mean(x * x, axis=-1, keepdims=True) + EPS)


def _swiglu(xn, wg_ref, wu_ref, wd_ref, row_scale=None):
    d_ff = wg_ref.shape[1]
    acc = None
    for start in range(0, d_ff, FF_CHUNK):
        sl = slice(start, min(start + FF_CHUNK, d_ff))
        g = jnp.dot(xn, wg_ref[:, sl], preferred_element_type=F32)
        u = jnp.dot(xn, wu_ref[:, sl], preferred_element_type=F32)
        if row_scale is not None:
            g = g * row_scale
            u = u * row_scale
        a = (g * jax.nn.sigmoid(g) * u).astype(BF16)
        d = jnp.dot(a, wd_ref[sl, :], preferred_element_type=F32)
        acc = d if acc is None else acc + d
    return acc


def _rope_block(x, cos, sin_signed, first_half):
    partner = jnp.where(first_half, pltpu.roll(x, LANES - 32, 1), pltpu.roll(x, 32, 1))
    return x * cos + partner * sin_signed


def _ffn_in_kernel(x_ref, cos_ref, sin_ref, n1_ref, wg_ref, wu_ref, wd_ref, n2_ref, win_ref,
                   h_ref, qa_ref, ka_ref, va_ref, qb_ref, kb_ref, vb_ref, *, v_transposed):
    x = x_ref[...]
    xg, x_scale = _rms_split(x, n1_ref[...])
    h = x + 0.5 * _swiglu(xg, wg_ref, wu_ref, wd_ref, x_scale)
    h_ref[...] = h
    u, h_scale = _rms_split(h, n2_ref[...])

    rows = x.shape[0]
    cos = cos_ref[...]
    sin_signed = sin_ref[...]
    lane = lax.broadcasted_iota(jnp.int32, (rows, LANES), 1)
    first_half = (lane % HEAD_DIM) < (HEAD_DIM // 2)
    low = lane < HEAD_DIM

    def project(col0, width):
        return jnp.dot(u, win_ref[:, col0:col0 + width], preferred_element_type=F32) * h_scale

    def store_rope(dst_ref, z, scale):
        for i in range(z.shape[1] // LANES):
            blk = _rope_block(z[:, i * LANES:(i + 1) * LANES], cos, sin_signed, first_half)
            if scale != 1.0:
                blk = blk * scale
            dst_ref[:, i * LANES:(i + 1) * LANES] = blk.astype(dst_ref.dtype)

    def store_padded(dst_ref, z):
        swapped = pltpu.roll(z, HEAD_DIM, 1)
        zero = jnp.zeros_like(z)
        parts = (jnp.where(low, z, zero), jnp.where(low, zero, swapped),
                 jnp.where(low, swapped, zero), jnp.where(low, zero, z))
        for i, part in enumerate(parts):
            dst_ref[:, i * LANES:(i + 1) * LANES] = part.astype(dst_ref.dtype)

    def store_padded_transposed(dst_ref, z):
        zt = z.T
        swapped = jnp.concatenate([zt[HEAD_DIM:], zt[:HEAD_DIM]], axis=0)
        top = lax.broadcasted_iota(jnp.int32, zt.shape, 0) < HEAD_DIM
        zero = jnp.zeros_like(zt)
        parts = (jnp.where(top, zt, zero), jnp.where(top, zero, swapped),
                 jnp.where(top, swapped, zero), jnp.where(top, zero, zt))
        for i, part in enumerate(parts):
            dst_ref[i * LANES:(i + 1) * LANES, :] = part.astype(dst_ref.dtype)

    store_rope(qa_ref, project(0, 512), QK_SCALE * LOG2E)
    store_rope(ka_ref, project(512, 512), 1.0)
    store_rope(qb_ref, project(1536, 512), QK_SCALE * LOG2E)
    zkv = project(2048, 2 * LANES)
    store_padded(kb_ref, _rope_block(zkv[:, :LANES], cos, sin_signed, first_half))
    zva = project(1024, 512)
    zvb = zkv[:, LANES:]
    if v_transposed:
        for i in range(zva.shape[1] // LANES):
            va_ref[i * LANES:(i + 1) * LANES, :] = zva[:, i * LANES:(i + 1) * LANES].T.astype(va_ref.dtype)
        store_padded_transposed(vb_ref, zvb)
    else:
        va_ref[...] = zva.astype(va_ref.dtype)
        store_padded(vb_ref, zvb)


def _const_spec(shape):
    zeros = (0,) * len(shape)
    return pl.BlockSpec(shape, lambda *_: zeros, pipeline_mode=pl.Buffered(1))


def _ffn_in(x, cos, sin_signed, n1, wg, wu, wd, n2, win, row_tile, v_transposed):
    rows, d = x.shape
    pos_tiles = cos.shape[0] // row_tile
    row = lambda i: (i, 0)
    out_w = 512
    wide = pl.BlockSpec((row_tile, out_w), row)
    wide_shape = jax.ShapeDtypeStruct((rows, out_w), BF16)
    v_spec = pl.BlockSpec((out_w, row_tile), lambda i: (0, i)) if v_transposed else wide
    v_shape = jax.ShapeDtypeStruct((out_w, rows), BF16) if v_transposed else wide_shape
    return pl.pallas_call(
        functools.partial(_ffn_in_kernel, v_transposed=v_transposed),
        grid=(rows // row_tile,),
        in_specs=[
            pl.BlockSpec((row_tile, d), row),
            pl.BlockSpec((row_tile, LANES), lambda i: (i % pos_tiles, 0)),
            pl.BlockSpec((row_tile, LANES), lambda i: (i % pos_tiles, 0)),
            _const_spec(n1.shape), _const_spec(wg.shape), _const_spec(wu.shape), _const_spec(wd.shape),
            _const_spec(n2.shape), _const_spec(win.shape),
        ],
        out_specs=[pl.BlockSpec((row_tile, d), row), wide, wide, v_spec, wide, wide, v_spec],
        out_shape=[jax.ShapeDtypeStruct((rows, d), F32), wide_shape, wide_shape, v_shape, wide_shape, wide_shape,
                   v_shape],
        compiler_params=pltpu.CompilerParams(dimension_semantics=("parallel",),
                                             vmem_limit_bytes=VMEM_LIMIT_ROWWISE),
        name="ffn_in",
    )(x, cos, sin_signed, n1, wg, wu, wd, n2, win)


def _diff_attn_kernel(lq1_ref, lk1_ref, lq2_ref, lk2_ref, gain_ref, q_ref, km_ref, vmt_ref, k_ref, vt_ref,
                      o_ref, qz_sc, sa_sc, sb_sc, sm_sc, m_sc, acc_sc):
    lam = (jnp.exp(jnp.sum(lq1_ref[...] * lk1_ref[...], keepdims=True))
           - jnp.exp(jnp.sum(lq2_ref[...] * lk2_ref[...], keepdims=True)) + LAMBDA_INIT)
    tq, tk = DIFF_TQ, DIFF_TK
    vdim = vt_ref.shape[0]
    nq = q_ref.shape[0] // tq
    nk = vt_ref.shape[1] // tk

    def with_ones(vt):
        return jnp.concatenate([vt, jnp.ones((BF16_SUBLANES, vt.shape[1]), BF16)], axis=0)

    def load_q(i, slot):
        q = q_ref[pl.ds(pl.multiple_of(i * tq, tq), tq), :]
        lane = lax.broadcasted_iota(jnp.int32, q.shape, 1)
        zero = jnp.zeros_like(q)
        qz_sc[slot, 0] = jnp.where(lane < HEAD_DIM, q, zero)
        qz_sc[slot, 1] = jnp.where(lane < HEAD_DIM, zero, q)

    def produce(t, s_sc, slot):
        kt = k_ref[pl.ds(pl.multiple_of(t * tk, tk), tk), :]
        maxima = []
        for c in range(2):
            s = lax.dot_general(kt, qz_sc[slot, c], _NT, preferred_element_type=F32)
            s_sc[c] = s
            maxima.append(jnp.max(s, axis=0, keepdims=True))
        return tuple(maxima)

    def fold(c, s_ref, s_max, vt):
        m_prev = m_sc[c]
        m_new = jnp.maximum(m_prev, s_max)
        alpha = jnp.exp2(m_prev - m_new)
        m_sc[c] = m_new
        for n0 in range(0, tq, MXU_WIDTH):
            cols = slice(n0, n0 + MXU_WIDTH)
            p = jnp.exp2(s_ref[:, cols] - m_new[:, cols])
            acc_sc[c, :, cols] = (alpha[:, cols] * acc_sc[c, :, cols]
                                  + jnp.dot(vt, p.astype(BF16), preferred_element_type=F32))

    def step(t, s_sc, maxima, t_next, s_next_sc, slot_next):
        vt = with_ones(vt_ref[:, pl.ds(pl.multiple_of(t * tk, tk), tk)])
        stats = []
        for c in range(2):
            m_prev = m_sc[c]
            m_new = jnp.maximum(m_prev, maxima[c])
            stats.append((m_new, jnp.exp2(m_prev - m_new)))
            m_sc[c] = m_new
        kt_next = k_ref[pl.ds(pl.multiple_of(t_next * tk, tk), tk), :]
        next_max = [[], []]
        for c in range(2):
            m_new, alpha = stats[c]
            for n0 in range(0, tq, MXU_WIDTH):
                cols = slice(n0, n0 + MXU_WIDTH)
                s = lax.dot_general(kt_next, qz_sc[slot_next, c, cols, :], _NT, preferred_element_type=F32)
                s_next_sc[c, :, cols] = s
                next_max[c].append(jnp.max(s, axis=0, keepdims=True))
                p = jnp.exp2(s_sc[c, :, cols] - m_new[:, cols])
                acc_sc[c, :, cols] = (alpha[:, cols] * acc_sc[c, :, cols]
                                      + jnp.dot(vt, p.astype(BF16), preferred_element_type=F32))
        return tuple(jnp.concatenate(parts, axis=1) for parts in next_max)

    def meta_scores(slot):
        for c in range(2):
            sm_sc[c] = lax.dot_general(km_ref[...], qz_sc[slot, c], _NT, preferred_element_type=F32)

    def consume_meta():
        vmt = with_ones(vmt_ref[...])
        for c in range(2):
            fold(c, sm_sc.at[c], jnp.max(sm_sc[c], axis=0, keepdims=True), vmt)

    def finalize(i):
        def normalised(c):
            acc = acc_sc[c]
            return acc[:vdim] * (1.0 / acc[vdim:vdim + 1])

        a = normalised(0) - lam * normalised(1)
        ms = jnp.mean(a * a, axis=0, keepdims=True)
        y = a * lax.rsqrt(ms + EPS) * gain_ref[...] * (1.0 - LAMBDA_INIT)
        o_ref[pl.ds(pl.multiple_of(i * tq, tq), tq), :] = y.T.astype(o_ref.dtype)

    def query_tile(i, maxima_a):
        slot = i % 2
        for c in range(2):
            m_sc[c] = jnp.full(m_sc.shape[1:], NEG, F32)
            acc_sc[c] = jnp.zeros(acc_sc.shape[1:], F32)

        def pair(j, maxima_a):
            t = 2 * j
            maxima_b = step(t, sa_sc, maxima_a, t + 1, sb_sc, slot)
            return step(t + 1, sb_sc, maxima_b, t + 2, sa_sc, slot)

        maxima_a = lax.fori_loop(0, nk // 2 - 1, pair, maxima_a)
        meta_scores(slot)
        maxima_b = step(nk - 2, sa_sc, maxima_a, nk - 1, sb_sc, slot)
        load_q(jnp.minimum(i + 1, nq - 1), 1 - slot)
        maxima_next = step(nk - 1, sb_sc, maxima_b, 0, sa_sc, 1 - slot)
        consume_meta()
        finalize(i)
        return maxima_next

    load_q(0, 0)
    lax.fori_loop(0, nq, query_tile, produce(0, sa_sc, 0))


def _diff_attn(lams, gain_col, qa, ka_meta, vat_meta, ka, vat, batch):
    rows = qa.shape[0]
    seq = rows // batch
    vdim = 2 * HEAD_DIM
    tq, tk = DIFF_TQ, DIFF_TK
    assert (seq // tk) % 2 == 0
    lam_spec = pl.BlockSpec((1, HEAD_DIM), lambda b, h: (0, 0))
    return pl.pallas_call(
        _diff_attn_kernel,
        grid=(batch, DIFF_HEADS),
        in_specs=[
            lam_spec, lam_spec, lam_spec, lam_spec,
            pl.BlockSpec((vdim, 1), lambda b, h: (0, 0)),
            pl.BlockSpec((seq, LANES), lambda b, h: (b, h)),
            pl.BlockSpec((N_META, LANES), lambda b, h: (0, h)),
            pl.BlockSpec((vdim, N_META), lambda b, h: (h, 0)),
            pl.BlockSpec((seq, LANES), lambda b, h: (b, h)),
            pl.BlockSpec((vdim, seq), lambda b, h: (h, b)),
        ],
        out_specs=pl.BlockSpec((seq, LANES), lambda b, h: (b, h)),
        out_shape=jax.ShapeDtypeStruct((rows, DIFF_HEADS * vdim), BF16),
        scratch_shapes=[pltpu.VMEM((2, 2, tq, LANES), BF16),
                        pltpu.VMEM((2, tk, tq), F32), pltpu.VMEM((2, tk, tq), F32),
                        pltpu.VMEM((2, N_META, tq), F32),
                        pltpu.VMEM((2, 1, tq), F32),
                        pltpu.VMEM((2, vdim + BF16_SUBLANES, tq), F32)],
        compiler_params=pltpu.CompilerParams(dimension_semantics=("parallel", "parallel"),
                                             vmem_limit_bytes=VMEM_LIMIT_ATTN),
        name="diff_attn",
    )(*lams, gain_col, qa, ka_meta, vat_meta, ka, vat)


def _win_attn_kernel(sink_ref, q_ref, km_ref, vmt_ref, k_ref, vt_ref, o_ref,
                     delta_sc, sa_sc, sb_sc, sma_sc, smb_sc):
    g = pl.program_id(1)
    blk = pl.program_id(2)
    seq = k_ref.shape[0]
    tq, span = WIN_TQ, WIN_SPAN
    n = 2 * tq
    nt = q_ref.shape[0] // tq
    delta_sc[...] = (lax.broadcasted_iota(jnp.int32, (span, tq), 0)
                     - lax.broadcasted_iota(jnp.int32, (span, tq), 1))
    first_pair = lax.broadcasted_iota(jnp.int32, (1, n), 1) < tq
    sinks = [jnp.where(first_pair, sink_ref[g * WIN_GROUP + par], sink_ref[g * WIN_GROUP + 2 + par]) * LOG2E
             for par in range(2)]

    ones_row = (HEAD_DIM, 0)
    top_half = lax.broadcasted_iota(jnp.int32, (LANES, n), 0) < HEAD_DIM

    def key_start(i):
        t0 = blk * (nt * tq) + i * tq
        return t0, pl.multiple_of(jnp.clip(t0 - WINDOW, 0, seq - span), WINDOW)

    def produce_pieces(i, s_sc, sm_sc):
        t0, ks = key_start(i)
        off = ks - t0
        rows = pl.ds(pl.multiple_of(i * tq, tq), tq)

        def piece(par):
            delta = delta_sc[...]
            valid = (delta >= -WINDOW - off) & (delta <= WINDOW - off)
            q_cat = jnp.concatenate([q_ref[rows, :LANES], q_ref[rows, LANES:]], axis=0)
            part = slice(par * LANES, (par + 1) * LANES)
            k_cat = jnp.concatenate([k_ref[pl.ds(ks, span), part], km_ref[:, part]], axis=0)
            s_all = lax.dot_general(k_cat, q_cat, _NT, preferred_element_type=F32)
            s = s_all[:span]
            s = jnp.concatenate([jnp.where(valid, s[:, :tq], NEG), jnp.where(valid, s[:, tq:], NEG)], axis=1)
            sm = s_all[span:]
            s_sc[par] = s
            sm_sc[par] = sm
            return jnp.maximum(jnp.maximum(jnp.max(s, axis=0, keepdims=True),
                                           jnp.max(sm, axis=0, keepdims=True)), sinks[par])

        return [functools.partial(piece, par) for par in range(2)]

    def consume_pieces(i, s_sc, sm_sc, maxima):
        _, ks = key_start(i)

        def with_ones(vt, par):
            r = ones_row[par]
            ones = jnp.ones((BF16_SUBLANES, vt.shape[1]), BF16)
            return jnp.concatenate(([vt[:r]] if r else []) + [ones, vt[r + BF16_SUBLANES:]], axis=0)

        def piece(par):
            part = slice(par * LANES, (par + 1) * LANES)
            m = maxima[par]
            p = jnp.exp2(s_sc[par] - m)
            pm = jnp.exp2(sm_sc[par] - m)
            o = (jnp.dot(with_ones(vt_ref[part, pl.ds(ks, span)], par), p.astype(BF16),
                         preferred_element_type=F32)
                 + jnp.dot(with_ones(vmt_ref[part, :], par), pm.astype(BF16), preferred_element_type=F32))
            l = o[ones_row[par]:ones_row[par] + 1] + jnp.exp2(sinks[par] - m)
            return o * (1.0 / l)

        def store(outs):
            o_t = jnp.where(top_half, outs[0], outs[1])
            rows = pl.ds(pl.multiple_of(i * tq, tq), tq)
            for pair in range(WIN_GROUP // 2):
                o_ref[rows, pair * LANES:(pair + 1) * LANES] = (
                    o_t[:, pair * tq:(pair + 1) * tq].T.astype(o_ref.dtype))

        return [functools.partial(piece, par) for par in range(2)], store

    def produce(i, s_sc, sm_sc):
        return tuple(piece() for piece in produce_pieces(i, s_sc, sm_sc))

    def consume(i, s_sc, sm_sc, maxima):
        pieces, store = consume_pieces(i, s_sc, sm_sc, maxima)
        store([piece() for piece in pieces])

    def step(i, s_sc, sm_sc, maxima, i_next, s_next_sc, sm_next_sc):
        makers = produce_pieces(i_next, s_next_sc, sm_next_sc)
        users, store = consume_pieces(i, s_sc, sm_sc, maxima)
        next_maxima, outs = [], []
        for make, use in zip(makers, users):
            next_maxima.append(make())
            outs.append(use())
        store(outs)
        return tuple(next_maxima)

    def pair_of_tiles(j, maxima_a):
        i = 2 * j
        maxima_b = step(i, sa_sc, sma_sc, maxima_a, i + 1, sb_sc, smb_sc)
        return step(i + 1, sb_sc, smb_sc, maxima_b, i + 2, sa_sc, sma_sc)

    maxima_a = lax.fori_loop(0, nt // 2 - 1, pair_of_tiles, produce(0, sa_sc, sma_sc), unroll=5)
    maxima_b = step(nt - 2, sa_sc, sma_sc, maxima_a, nt - 1, sb_sc, smb_sc)
    consume(nt - 1, sb_sc, smb_sc, maxima_b)


def _win_attn(sink, qb, kb_meta, vbt_meta, kb, vbt, batch):
    rows = qb.shape[0]
    seq = rows // batch
    nblk = seq // WIN_QBLOCK
    assert (WIN_QBLOCK // WIN_TQ) % 2 == 0
    width = 2 * LANES
    n = 2 * WIN_TQ
    return pl.pallas_call(
        _win_attn_kernel,
        grid=(batch, WIN_KV_HEADS, nblk),
        in_specs=[
            pl.BlockSpec(memory_space=pltpu.SMEM),
            pl.BlockSpec((WIN_QBLOCK, width), lambda b, g, i: (b * nblk + i, g)),
            pl.BlockSpec((N_META, width), lambda b, g, i: (0, g)),
            pl.BlockSpec((width, N_META), lambda b, g, i: (g, 0)),
            pl.BlockSpec((seq, width), lambda b, g, i: (b, g)),
            pl.BlockSpec((width, seq), lambda b, g, i: (g, b)),
        ],
        out_specs=pl.BlockSpec((WIN_QBLOCK, width), lambda b, g, i: (b * nblk + i, g)),
        out_shape=jax.ShapeDtypeStruct((rows, WIN_KV_HEADS * width), BF16),
        scratch_shapes=[pltpu.VMEM((WIN_SPAN, WIN_TQ), jnp.int32),
                        pltpu.VMEM((2, WIN_SPAN, n), F32), pltpu.VMEM((2, WIN_SPAN, n), F32),
                        pltpu.VMEM((2, N_META, n), F32), pltpu.VMEM((2, N_META, n), F32)],
        compiler_params=pltpu.CompilerParams(dimension_semantics=("parallel", "parallel", "arbitrary"),
                                             vmem_limit_bytes=VMEM_LIMIT_ATTN),
        name="win_attn",
    )(sink, qb, kb_meta, vbt_meta, kb, vbt)


def _out_ffn_kernel(h_ref, oa_ref, ob_ref, wn_ref, wo_ref, n_ref, wg_ref, wu_ref, wd_ref, fn_ref, out_ref):
    half = oa_ref.shape[1]
    ob, ob_scale = _rms_split(ob_ref[...].astype(F32), wn_ref[...])
    mix = (jnp.dot(oa_ref[...], wo_ref[:half, :], preferred_element_type=F32)
           + jnp.dot(ob, wo_ref[half:, :], preferred_element_type=F32) * ob_scale)
    h = h_ref[...] + mix
    hg, h_scale = _rms_split(h, n_ref[...])
    h = h + 0.5 * _swiglu(hg, wg_ref, wu_ref, wd_ref, h_scale)
    out_ref[...] = _rms(h, fn_ref[...])


def _out_ffn(h1, oa, ob, wn, wo, n, wg, wu, wd, fn):
    rows, d = h1.shape
    half = oa.shape[1]
    row = lambda i: (i, 0)
    return pl.pallas_call(
        _out_ffn_kernel,
        grid=(rows // ROW_TILE,),
        in_specs=[
            pl.BlockSpec((ROW_TILE, d), row), pl.BlockSpec((ROW_TILE, half), row), pl.BlockSpec((ROW_TILE, half), row),
            _const_spec(wn.shape), _const_spec(wo.shape), _const_spec(n.shape),
            _const_spec(wg.shape), _const_spec(wu.shape), _const_spec(wd.shape), _const_spec(fn.shape),
        ],
        out_specs=pl.BlockSpec((ROW_TILE, d), row),
        out_shape=jax.ShapeDtypeStruct((rows, d), F32),
        compiler_params=pltpu.CompilerParams(dimension_semantics=("parallel",),
                                             vmem_limit_bytes=VMEM_LIMIT_ROWWISE),
        name="out_ffn",
    )(h1, oa, ob, wn, wo, n, wg, wu, wd, fn)


def _rope_tables(length):
    pos = jnp.arange(length, dtype=F32)
    inv = ROPE_THETA ** (-jnp.arange(0, HEAD_DIM, 2, dtype=F32) / HEAD_DIM)
    ang = pos[:, None] * inv[None, :]
    cos, sin = jnp.cos(ang), jnp.sin(ang)
    return jnp.tile(cos, (1, 4)), jnp.tile(jnp.concatenate([-sin, sin], axis=1), (1, 2))


def kernel(x, meta_tokens, ffn1_norm, ffn1_w_gate, ffn1_w_up, ffn1_w_down, mix_norm, w_in, lambda_q1, lambda_k1, lambda_q2, lambda_k2, diff_norm, win_sink, win_norm, w_out, ffn2_norm, ffn2_w_gate, ffn2_w_up, ffn2_w_down, final_norm):
    batch, seq, d = x.shape
    assert ffn1_norm.shape[0] == 1, "single layer only"
    assert seq % ROW_TILE == 0 and seq % DIFF_TQ == 0 and seq % DIFF_TK == 0 and seq % WIN_QBLOCK == 0

    cos, sin_signed = _rope_tables(N_META + seq)
    ffn_in_weights = (ffn1_norm, ffn1_w_gate[0].astype(BF16), ffn1_w_up[0].astype(BF16),
                      ffn1_w_down[0].astype(BF16), mix_norm, w_in[0].astype(BF16))

    meta = _ffn_in(meta_tokens.astype(x.dtype), cos[:N_META], sin_signed[:N_META], *ffn_in_weights, N_META,
                   v_transposed=False)
    real = _ffn_in(x.reshape(batch * seq, d), cos[N_META:], sin_signed[N_META:], *ffn_in_weights, ROW_TILE,
                   v_transposed=True)
    _, _, ka_m, va_m, _, kb_m, vb_m = meta
    h1, qa, ka, vat, qb, kb, vbt = real

    lams = (lambda_q1, lambda_k1, lambda_q2, lambda_k2)
    out_a = _diff_attn(lams, diff_norm.reshape(2 * HEAD_DIM, 1), qa, ka_m, va_m.T, ka, vat, batch)

    out_b = _win_attn(win_sink.reshape(-1), qb, kb_m, vb_m.T, kb, vbt, batch)

    out = _out_ffn(h1, out_a, out_b, win_norm, w_out[0].astype(BF16), ffn2_norm,
                   ffn2_w_gate[0].astype(BF16), ffn2_w_up[0].astype(BF16), ffn2_w_down[0].astype(BF16),
                   final_norm.reshape(1, d))
    return out.reshape(batch, seq, d)
```

```python
import functools
import math

import jax
import jax.numpy as jnp
from jax import lax
from jax.experimental import pallas as pl
from jax.experimental.pallas import tpu as pltpu

N_META = 16
HEAD_DIM = 64
DIFF_HEADS = 4
WIN_KV_HEADS = 2
WIN_GROUP = 4
WINDOW = 128
ROPE_THETA = 10000.0
EPS = 1e-6
NEG = -1e30
LAMBDA_INIT = 0.8 - 0.6 * math.exp(-0.3 * 0)
QK_SCALE = HEAD_DIM ** -0.5
LOG2E = math.log2(math.e)

LANES = 128
BF16_SUBLANES = 16
VMEM_LIMIT_ROWWISE = 56 * 1024 * 1024
VMEM_LIMIT_ATTN = 40 * 1024 * 1024

ROW_TILE = 512
OUT_ROW_TILE = 1024
MXU_WIDTH = 256
FF_CHUNK = 6 * MXU_WIDTH
DIFF_TQ = 1024
DIFF_TK = 1024
WIN_TQ = 128
WIN_SPAN = WIN_TQ + 2 * WINDOW
WIN_QBLOCK = 4096

F32 = jnp.float32
BF16 = jnp.bfloat16
_NT = (((1,), (1,)), ((), ()))


def _rms(x, gain):
    return x * lax.rsqrt(jnp.mean(x * x, axis=-1, keepdims=True) + EPS) * gain


def _rms_split(x, gain):
    return (x * gain).astype(BF16), lax.rsqrt(jnp.mean(x * x, axis=-1, keepdims=True) + EPS)


def _swiglu(xn, wg_ref, wu_ref, wd_ref, row_scale=None):
    d_ff = wg_ref.shape[1]
    acc = None
    for start in range(0, d_ff, FF_CHUNK):
        sl = slice(start, min(start + FF_CHUNK, d_ff))
        g = jnp.dot(xn, wg_ref[:, sl], preferred_element_type=F32)
        u = jnp.dot(xn, wu_ref[:, sl], preferred_element_type=F32)
        if row_scale is not None:
            g = g * row_scale
            u = u * row_scale
        a = (g * jax.nn.sigmoid(g) * u).astype(BF16)
        d = jnp.dot(a, wd_ref[sl, :], preferred_element_type=F32)
        acc = d if acc is None else acc + d
    return acc


def _rope_block(x, cos, sin_signed, first_half):
    partner = jnp.where(first_half, pltpu.roll(x, LANES - 32, 1), pltpu.roll(x, 32, 1))
    return x * cos + partner * sin_signed


def _ffn_in_kernel(x_ref, cos_ref, sin_ref, n1_ref, wg_ref, wu_ref, wd_ref, n2_ref, win_ref,
                   h_ref, qa_ref, ka_ref, va_ref, qb_ref, kb_ref, vb_ref, *, v_transposed):
    x = x_ref[...]
    xg, x_scale = _rms_split(x, n1_ref[...])
    h = x + 0.5 * _swiglu(xg, wg_ref, wu_ref, wd_ref, x_scale)
    h_ref[...] = h
    u, h_scale = _rms_split(h, n2_ref[...])

    rows = x.shape[0]
    cos = cos_ref[...]
    sin_signed = sin_ref[...]
    lane = lax.broadcasted_iota(jnp.int32, (rows, LANES), 1)
    first_half = (lane % HEAD_DIM) < (HEAD_DIM // 2)
    low = lane < HEAD_DIM

    def project(col0, width):
        return jnp.dot(u, win_ref[:, col0:col0 + width], preferred_element_type=F32) * h_scale

    def store_rope(dst_ref, z, scale):
        for i in range(z.shape[1] // LANES):
            blk = _rope_block(z[:, i * LANES:(i + 1) * LANES], cos, sin_signed, first_half)
            if scale != 1.0:
                blk = blk * scale
            dst_ref[:, i * LANES:(i + 1) * LANES] = blk.astype(dst_ref.dtype)

    def store_padded(dst_ref, z):
        swapped = pltpu.roll(z, HEAD_DIM, 1)
        zero = jnp.zeros_like(z)
        parts = (jnp.where(low, z, zero), jnp.where(low, zero, swapped),
                 jnp.where(low, swapped, zero), jnp.where(low, zero, z))
        for i, part in enumerate(parts):
            dst_ref[:, i * LANES:(i + 1) * LANES] = part.astype(dst_ref.dtype)

    def store_padded_transposed(dst_ref, z):
        zt = z.T
        swapped = jnp.concatenate([zt[HEAD_DIM:], zt[:HEAD_DIM]], axis=0)
        top = lax.broadcasted_iota(jnp.int32, zt.shape, 0) < HEAD_DIM
        zero = jnp.zeros_like(zt)
        parts = (jnp.where(top, zt, zero), jnp.where(top, zero, swapped),
                 jnp.where(top, swapped, zero), jnp.where(top, zero, zt))
        for i, part in enumerate(parts):
            dst_ref[i * LANES:(i + 1) * LANES, :] = part.astype(dst_ref.dtype)

    store_rope(qa_ref, project(0, 512), QK_SCALE * LOG2E)
    store_rope(ka_ref, project(512, 512), 1.0)
    store_rope(qb_ref, project(1536, 512), QK_SCALE * LOG2E)
    zkv = project(2048, 2 * LANES)
    store_padded(kb_ref, _rope_block(zkv[:, :LANES], cos, sin_signed, first_half))
    zva = project(1024, 512)
    zvb = zkv[:, LANES:]
    if v_transposed:
        for i in range(zva.shape[1] // LANES):
            va_ref[i * LANES:(i + 1) * LANES, :] = zva[:, i * LANES:(i + 1) * LANES].T.astype(va_ref.dtype)
        store_padded_transposed(vb_ref, zvb)
    else:
        va_ref[...] = zva.astype(va_ref.dtype)
        store_padded(vb_ref, zvb)


def _const_spec(shape):
    zeros = (0,) * len(shape)
    return pl.BlockSpec(shape, lambda *_: zeros, pipeline_mode=pl.Buffered(1))


def _ffn_in(x, cos, sin_signed, n1, wg, wu, wd, n2, win, row_tile, v_transposed):
    rows, d = x.shape
    pos_tiles = cos.shape[0] // row_tile
    row = lambda i: (i, 0)
    out_w = 512
    wide = pl.BlockSpec((row_tile, out_w), row)
    wide_shape = jax.ShapeDtypeStruct((rows, out_w), BF16)
    v_spec = pl.BlockSpec((out_w, row_tile), lambda i: (0, i)) if v_transposed else wide
    v_shape = jax.ShapeDtypeStruct((out_w, rows), BF16) if v_transposed else wide_shape
    return pl.pallas_call(
        functools.partial(_ffn_in_kernel, v_transposed=v_transposed),
        grid=(rows // row_tile,),
        in_specs=[
            pl.BlockSpec((row_tile, d), row),
            pl.BlockSpec((row_tile, LANES), lambda i: (i % pos_tiles, 0)),
            pl.BlockSpec((row_tile, LANES), lambda i: (i % pos_tiles, 0)),
            _const_spec(n1.shape), _const_spec(wg.shape), _const_spec(wu.shape), _const_spec(wd.shape),
            _const_spec(n2.shape), _const_spec(win.shape),
        ],
        out_specs=[pl.BlockSpec((row_tile, d), row), wide, wide, v_spec, wide, wide, v_spec],
        out_shape=[jax.ShapeDtypeStruct((rows, d), F32), wide_shape, wide_shape, v_shape, wide_shape, wide_shape,
                   v_shape],
        compiler_params=pltpu.CompilerParams(dimension_semantics=("parallel",),
                                             vmem_limit_bytes=VMEM_LIMIT_ROWWISE),
        name="ffn_in",
    )(x, cos, sin_signed, n1, wg, wu, wd, n2, win)


def _diff_attn_kernel(lq1_ref, lk1_ref, lq2_ref, lk2_ref, gain_ref, q_ref, km_ref, vmt_ref, k_ref, vt_ref,
                      o_ref, qz_sc, sa_sc, sb_sc, sm_sc, m_sc, acc_sc):
    lam = (jnp.exp(jnp.sum(lq1_ref[...] * lk1_ref[...], keepdims=True))
           - jnp.exp(jnp.sum(lq2_ref[...] * lk2_ref[...], keepdims=True)) + LAMBDA_INIT)
    tq, tk = DIFF_TQ, DIFF_TK
    vdim = vt_ref.shape[0]
    nq = q_ref.shape[0] // tq
    nk = vt_ref.shape[1] // tk

    def with_ones(vt):
        return jnp.concatenate([vt, jnp.ones((BF16_SUBLANES, vt.shape[1]), BF16)], axis=0)

    def load_q(i, slot):
        q = q_ref[pl.ds(pl.multiple_of(i * tq, tq), tq), :]
        lane = lax.broadcasted_iota(jnp.int32, q.shape, 1)
        zero = jnp.zeros_like(q)
        qz_sc[slot, 0] = jnp.where(lane < HEAD_DIM, q, zero)
        qz_sc[slot, 1] = jnp.where(lane < HEAD_DIM, zero, q)

    def produce(t, s_sc, slot):
        kt = k_ref[pl.ds(pl.multiple_of(t * tk, tk), tk), :]
        maxima = []
        for c in range(2):
            s = lax.dot_general(kt, qz_sc[slot, c], _NT, preferred_element_type=F32)
            s_sc[c] = s
            maxima.append(jnp.max(s, axis=0, keepdims=True))
        return tuple(maxima)

    def fold(c, s_ref, s_max, vt):
        m_prev = m_sc[c]
        m_new = jnp.maximum(m_prev, s_max)
        alpha = jnp.exp2(m_prev - m_new)
        m_sc[c] = m_new
        for n0 in range(0, tq, MXU_WIDTH):
            cols = slice(n0, n0 + MXU_WIDTH)
            p = jnp.exp2(s_ref[:, cols] - m_new[:, cols])
            acc_sc[c, :, cols] = (alpha[:, cols] * acc_sc[c, :, cols]
                                  + jnp.dot(vt, p.astype(BF16), preferred_element_type=F32))

    def step(t, s_sc, maxima, t_next, s_next_sc, slot_next):
        vt = with_ones(vt_ref[:, pl.ds(pl.multiple_of(t * tk, tk), tk)])
        stats = []
        for c in range(2):
            m_prev = m_sc[c]
            m_new = jnp.maximum(m_prev, maxima[c])
            stats.append((m_new, jnp.exp2(m_prev - m_new)))
            m_sc[c] = m_new
        kt_next = k_ref[pl.ds(pl.multiple_of(t_next * tk, tk), tk), :]
        next_max = [[], []]
        for c in range(2):
            m_new, alpha = stats[c]
            for n0 in range(0, tq, MXU_WIDTH):
                cols = slice(n0, n0 + MXU_WIDTH)
                s = lax.dot_general(kt_next, qz_sc[slot_next, c, cols, :], _NT, preferred_element_type=F32)
                s_next_sc[c, :, cols] = s
                next_max[c].append(jnp.max(s, axis=0, keepdims=True))
                p = jnp.exp2(s_sc[c, :, cols] - m_new[:, cols])
                acc_sc[c, :, cols] = (alpha[:, cols] * acc_sc[c, :, cols]
                                      + jnp.dot(vt, p.astype(BF16), preferred_element_type=F32))
        return tuple(jnp.concatenate(parts, axis=1) for parts in next_max)

    def meta_scores(slot):
        for c in range(2):
            sm_sc[c] = lax.dot_general(km_ref[...], qz_sc[slot, c], _NT, preferred_element_type=F32)

    def consume_meta():
        vmt = with_ones(vmt_ref[...])
        for c in range(2):
            fold(c, sm_sc.at[c], jnp.max(sm_sc[c], axis=0, keepdims=True), vmt)

    def finalize(i):
        def normalised(c):
            acc = acc_sc[c]
            return acc[:vdim] * (1.0 / acc[vdim:vdim + 1])

        a = normalised(0) - lam * normalised(1)
        ms = jnp.mean(a * a, axis=0, keepdims=True)
        y = a * lax.rsqrt(ms + EPS) * gain_ref[...] * (1.0 - LAMBDA_INIT)
        o_ref[pl.ds(pl.multiple_of(i * tq, tq), tq), :] = y.T.astype(o_ref.dtype)

    def query_tile(i, maxima_a):
        slot = i % 2
        for c in range(2):
            m_sc[c] = jnp.full(m_sc.shape[1:], NEG, F32)
            acc_sc[c] = jnp.zeros(acc_sc.shape[1:], F32)

        def pair(j, maxima_a):
            t = 2 * j
            maxima_b = step(t, sa_sc, maxima_a, t + 1, sb_sc, slot)
            return step(t + 1, sb_sc, maxima_b, t + 2, sa_sc, slot)

        maxima_a = lax.fori_loop(0, nk // 2 - 1, pair, maxima_a)
        meta_scores(slot)
        maxima_b = step(nk - 2, sa_sc, maxima_a, nk - 1, sb_sc, slot)
        load_q(jnp.minimum(i + 1, nq - 1), 1 - slot)
        maxima_next = step(nk - 1, sb_sc, maxima_b, 0, sa_sc, 1 - slot)
        consume_meta()
        finalize(i)
        return maxima_next

    load_q(0, 0)
    lax.fori_loop(0, nq, query_tile, produce(0, sa_sc, 0))


def _diff_attn(lams, gain_col, qa, ka_meta, vat_meta, ka, vat, batch):
    rows = qa.shape[0]
    seq = rows // batch
    vdim = 2 * HEAD_DIM
    tq, tk = DIFF_TQ, DIFF_TK
    assert (seq // tk) % 2 == 0
    lam_spec = pl.BlockSpec((1, HEAD_DIM), lambda b, h: (0, 0))
    return pl.pallas_call(
        _diff_attn_kernel,
        grid=(batch, DIFF_HEADS),
        in_specs=[
            lam_spec, lam_spec, lam_spec, lam_spec,
            pl.BlockSpec((vdim, 1), lambda b, h: (0, 0)),
            pl.BlockSpec((seq, LANES), lambda b, h: (b, h)),
            pl.BlockSpec((N_META, LANES), lambda b, h: (0, h)),
            pl.BlockSpec((vdim, N_META), lambda b, h: (h, 0)),
            pl.BlockSpec((seq, LANES), lambda b, h: (b, h)),
            pl.BlockSpec((vdim, seq), lambda b, h: (h, b)),
        ],
        out_specs=pl.BlockSpec((seq, LANES), lambda b, h: (b, h)),
        out_shape=jax.ShapeDtypeStruct((rows, DIFF_HEADS * vdim), BF16),
        scratch_shapes=[pltpu.VMEM((2, 2, tq, LANES), BF16),
                        pltpu.VMEM((2, tk, tq), F32), pltpu.VMEM((2, tk, tq), F32),
                        pltpu.VMEM((2, N_META, tq), F32),
                        pltpu.VMEM((2, 1, tq), F32),
                        pltpu.VMEM((2, vdim + BF16_SUBLANES, tq), F32)],
        compiler_params=pltpu.CompilerParams(dimension_semantics=("parallel", "parallel"),
                                             vmem_limit_bytes=VMEM_LIMIT_ATTN),
        name="diff_attn",
    )(*lams, gain_col, qa, ka_meta, vat_meta, ka, vat)


def _win_attn_kernel(sink_ref, q_ref, km_ref, vmt_ref, k_ref, vt_ref, o_ref,
                     delta_sc, sa_sc, sb_sc, sma_sc, smb_sc):
    g = pl.program_id(1)
    blk = pl.program_id(2)
    seq = k_ref.shape[0]
    tq, span = WIN_TQ, WIN_SPAN
    n = 2 * tq
    nt = q_ref.shape[0] // tq
    delta_sc[...] = (lax.broadcasted_iota(jnp.int32, (span, tq), 0)
                     - lax.broadcasted_iota(jnp.int32, (span, tq), 1))
    first_pair = lax.broadcasted_iota(jnp.int32, (1, n), 1) < tq
    sinks = [jnp.where(first_pair, sink_ref[g * WIN_GROUP + par], sink_ref[g * WIN_GROUP + 2 + par]) * LOG2E
             for par in range(2)]

    ones_row = (HEAD_DIM, 0)
    top_half = lax.broadcasted_iota(jnp.int32, (LANES, n), 0) < HEAD_DIM

    def key_start(i):
        t0 = blk * (nt * tq) + i * tq
        return t0, pl.multiple_of(jnp.clip(t0 - WINDOW, 0, seq - span), WINDOW)

    def produce_pieces(i, s_sc, sm_sc):
        t0, ks = key_start(i)
        off = ks - t0
        rows = pl.ds(pl.multiple_of(i * tq, tq), tq)

        def piece(par):
            delta = delta_sc[...]
            valid = (delta >= -WINDOW - off) & (delta <= WINDOW - off)
            q_cat = jnp.concatenate([q_ref[rows, :LANES], q_ref[rows, LANES:]], axis=0)
            part = slice(par * LANES, (par + 1) * LANES)
            k_cat = jnp.concatenate([k_ref[pl.ds(ks, span), part], km_ref[:, part]], axis=0)
            s_all = lax.dot_general(k_cat, q_cat, _NT, preferred_element_type=F32)
            s = s_all[:span]
            s = jnp.concatenate([jnp.where(valid, s[:, :tq], NEG), jnp.where(valid, s[:, tq:], NEG)], axis=1)
            sm = s_all[span:]
            s_sc[par] = s
            sm_sc[par] = sm
            return jnp.maximum(jnp.maximum(jnp.max(s, axis=0, keepdims=True),
                                           jnp.max(sm, axis=0, keepdims=True)), sinks[par])

        return [functools.partial(piece, par) for par in range(2)]

    def consume_pieces(i, s_sc, sm_sc, maxima):
        _, ks = key_start(i)

        def with_ones(vt, par):
            r = ones_row[par]
            ones = jnp.ones((BF16_SUBLANES, vt.shape[1]), BF16)
            return jnp.concatenate(([vt[:r]] if r else []) + [ones, vt[r + BF16_SUBLANES:]], axis=0)

        def piece(par):
            part = slice(par * LANES, (par + 1) * LANES)
            m = maxima[par]
            p = jnp.exp2(s_sc[par] - m)
            pm = jnp.exp2(sm_sc[par] - m)
            o = (jnp.dot(with_ones(vt_ref[part, pl.ds(ks, span)], par), p.astype(BF16),
                         preferred_element_type=F32)
                 + jnp.dot(with_ones(vmt_ref[part, :], par), pm.astype(BF16), preferred_element_type=F32))
            l = o[ones_row[par]:ones_row[par] + 1] + jnp.exp2(sinks[par] - m)
            return o * (1.0 / l)

        def store(outs):
            o_t = jnp.where(top_half, outs[0], outs[1])
            rows = pl.ds(pl.multiple_of(i * tq, tq), tq)
            for pair in range(WIN_GROUP // 2):
                o_ref[rows, pair * LANES:(pair + 1) * LANES] = (
                    o_t[:, pair * tq:(pair + 1) * tq].T.astype(o_ref.dtype))

        return [functools.partial(piece, par) for par in range(2)], store

    def produce(i, s_sc, sm_sc):
        return tuple(piece() for piece in produce_pieces(i, s_sc, sm_sc))

    def consume(i, s_sc, sm_sc, maxima):
        pieces, store = consume_pieces(i, s_sc, sm_sc, maxima)
        store([piece() for piece in pieces])

    def step(i, s_sc, sm_sc, maxima, i_next, s_next_sc, sm_next_sc):
        makers = produce_pieces(i_next, s_next_sc, sm_next_sc)
        users, store = consume_pieces(i, s_sc, sm_sc, maxima)
        next_maxima, outs = [], []
        for make, use in zip(makers, users):
            next_maxima.append(make())
            outs.append(use())
        store(outs)
        return tuple(next_maxima)

    def pair_of_tiles(j, maxima_a):
        i = 2 * j
        maxima_b = step(i, sa_sc, sma_sc, maxima_a, i + 1, sb_sc, smb_sc)
        return step(i + 1, sb_sc, smb_sc, maxima_b, i + 2, sa_sc, sma_sc)

    maxima_a = lax.fori_loop(0, nt // 2 - 1, pair_of_tiles, produce(0, sa_sc, sma_sc), unroll=5)
    maxima_b = step(nt - 2, sa_sc, sma_sc, maxima_a, nt - 1, sb_sc, smb_sc)
    consume(nt - 1, sb_sc, smb_sc, maxima_b)


def _win_attn(sink, qb, kb_meta, vbt_meta, kb, vbt, batch):
    rows = qb.shape[0]
    seq = rows // batch
    nblk = seq // WIN_QBLOCK
    assert (WIN_QBLOCK // WIN_TQ) % 2 == 0
    width = 2 * LANES
    n = 2 * WIN_TQ
    return pl.pallas_call(
        _win_attn_kernel,
        grid=(batch, WIN_KV_HEADS, nblk),
        in_specs=[
            pl.BlockSpec(memory_space=pltpu.SMEM),
            pl.BlockSpec((WIN_QBLOCK, width), lambda b, g, i: (b * nblk + i, g)),
            pl.BlockSpec((N_META, width), lambda b, g, i: (0, g)),
            pl.BlockSpec((width, N_META), lambda b, g, i: (g, 0)),
            pl.BlockSpec((seq, width), lambda b, g, i: (b, g)),
            pl.BlockSpec((width, seq), lambda b, g, i: (g, b)),
        ],
        out_specs=pl.BlockSpec((WIN_QBLOCK, width), lambda b, g, i: (b * nblk + i, g)),
        out_shape=jax.ShapeDtypeStruct((rows, WIN_KV_HEADS * width), BF16),
        scratch_shapes=[pltpu.VMEM((WIN_SPAN, WIN_TQ), jnp.int32),
                        pltpu.VMEM((2, WIN_SPAN, n), F32), pltpu.VMEM((2, WIN_SPAN, n), F32),
                        pltpu.VMEM((2, N_META, n), F32), pltpu.VMEM((2, N_META, n), F32)],
        compiler_params=pltpu.CompilerParams(dimension_semantics=("parallel", "parallel", "arbitrary"),
                                             vmem_limit_bytes=VMEM_LIMIT_ATTN),
        name="win_attn",
    )(sink, qb, kb_meta, vbt_meta, kb, vbt)


def _out_ffn_kernel(h_ref, oa_ref, ob_ref, wn_ref, wo_ref, n_ref, wg_ref, wu_ref, wd_ref, fn_ref, out_ref):
    half = oa_ref.shape[1]
    ob, ob_scale = _rms_split(ob_ref[...].astype(F32), wn_ref[...])
    mix = (jnp.dot(oa_ref[...], wo_ref[:half, :], preferred_element_type=F32)
           + jnp.dot(ob, wo_ref[half:, :], preferred_element_type=F32) * ob_scale)
    h = h_ref[...] + mix
    hg, h_scale = _rms_split(h, n_ref[...])
    h = h + 0.5 * _swiglu(hg, wg_ref, wu_ref, wd_ref, h_scale)
    out_ref[...] = _rms(h, fn_ref[...])


def _out_ffn(h1, oa, ob, wn, wo, n, wg, wu, wd, fn):
    rows, d = h1.shape
    half = oa.shape[1]
    row = lambda i: (i, 0)
    return pl.pallas_call(
        _out_ffn_kernel,
        grid=(rows // OUT_ROW_TILE,),
        in_specs=[
            pl.BlockSpec((OUT_ROW_TILE, d), row), pl.BlockSpec((OUT_ROW_TILE, half), row),
            pl.BlockSpec((OUT_ROW_TILE, half), row),
            _const_spec(wn.shape), _const_spec(wo.shape), _const_spec(n.shape),
            _const_spec(wg.shape), _const_spec(wu.shape), _const_spec(wd.shape), _const_spec(fn.shape),
        ],
        out_specs=pl.BlockSpec((OUT_ROW_TILE, d), row),
        out_shape=jax.ShapeDtypeStruct((rows, d), F32),
        compiler_params=pltpu.CompilerParams(dimension_semantics=("parallel",),
                                             vmem_limit_bytes=VMEM_LIMIT_ROWWISE),
        name="out_ffn",
    )(h1, oa, ob, wn, wo, n, wg, wu, wd, fn)


def _rope_tables(length):
    pos = jnp.arange(length, dtype=F32)
    inv = ROPE_THETA ** (-jnp.arange(0, HEAD_DIM, 2, dtype=F32) / HEAD_DIM)
    ang = pos[:, None] * inv[None, :]
    cos, sin = jnp.cos(ang), jnp.sin(ang)
    return jnp.tile(cos, (1, 4)), jnp.tile(jnp.concatenate([-sin, sin], axis=1), (1, 2))


def kernel(x, meta_tokens, ffn1_norm, ffn1_w_gate, ffn1_w_up, ffn1_w_down, mix_norm, w_in, lambda_q1, lambda_k1, lambda_q2, lambda_k2, diff_norm, win_sink, win_norm, w_out, ffn2_norm, ffn2_w_gate, ffn2_w_up, ffn2_w_down, final_norm):
    batch, seq, d = x.shape
    assert ffn1_norm.shape[0] == 1, "single layer only"
    assert seq % ROW_TILE == 0 and seq % DIFF_TQ == 0 and seq % DIFF_TK == 0 and seq % WIN_QBLOCK == 0

    cos, sin_signed = _rope_tables(N_META + seq)
    ffn_in_weights = (ffn1_norm, ffn1_w_gate[0].astype(BF16), ffn1_w_up[0].astype(BF16),
                      ffn1_w_down[0].astype(BF16), mix_norm, w_in[0].astype(BF16))

    meta = _ffn_in(meta_tokens.astype(x.dtype), cos[:N_META], sin_signed[:N_META], *ffn_in_weights, N_META,
                   v_transposed=False)
    real = _ffn_in(x.reshape(batch * seq, d), cos[N_META:], sin_signed[N_META:], *ffn_in_weights, ROW_TILE,
                   v_transposed=True)
    _, _, ka_m, va_m, _, kb_m, vb_m = meta
    h1, qa, ka, vat, qb, kb, vbt = real

    lams = (lambda_q1, lambda_k1, lambda_q2, lambda_k2)
    out_a = _diff_attn(lams, diff_norm.reshape(2 * HEAD_DIM, 1), qa, ka_m, va_m.T, ka, vat, batch)

    out_b = _win_attn(win_sink.reshape(-1), qb, kb_m, vb_m.T, kb, vbt, batch)

    out = _out_ffn(h1, out_a, out_b, win_norm, w_out[0].astype(BF16), ffn2_norm,
                   ffn2_w_gate[0].astype(BF16), ffn2_w_up[0].astype(BF16), ffn2_w_down[0].astype(BF16),
                   final_norm.reshape(1, d))
    return out.reshape(batch, seq, d)
```

```python
import functools
import math

import jax
import jax.numpy as jnp
from jax import lax
from jax.experimental import pallas as pl
from jax.experimental.pallas import tpu as pltpu

N_META = 16
HEAD_DIM = 64
DIFF_HEADS = 4
WIN_KV_HEADS = 2
WIN_GROUP = 4
WINDOW = 128
ROPE_THETA = 10000.0
EPS = 1e-6
NEG = -1e30
LAMBDA_INIT = 0.8 - 0.6 * math.exp(-0.3 * 0)
QK_SCALE = HEAD_DIM ** -0.5
LOG2E = math.log2(math.e)

LANES = 128
BF16_SUBLANES = 16
VMEM_LIMIT_ROWWISE = 56 * 1024 * 1024
VMEM_LIMIT_ATTN = 40 * 1024 * 1024

ROW_TILE = 512
OUT_ROW_TILE = 1024
MXU_WIDTH = 256
FF_CHUNK = 6 * MXU_WIDTH
DIFF_TQ = 1024
DIFF_TK = 1024
WIN_TQ = 128
WIN_SPAN = WIN_TQ + 2 * WINDOW
WIN_QBLOCK = 8192

F32 = jnp.float32
BF16 = jnp.bfloat16
_NT = (((1,), (1,)), ((), ()))


def _rms(x, gain):
    return x * lax.rsqrt(jnp.mean(x * x, axis=-1, keepdims=True) + EPS) * gain


def _rms_split(x, gain):
    return (x * gain).astype(BF16), lax.rsqrt(jnp.mean(x * x, axis=-1, keepdims=True) + EPS)


def _swiglu(xn, wg_ref, wu_ref, wd_ref, row_scale=None):
    d_ff = wg_ref.shape[1]
    acc = None
    for start in range(0, d_ff, FF_CHUNK):
        sl = slice(start, min(start + FF_CHUNK, d_ff))
        g = jnp.dot(xn, wg_ref[:, sl], preferred_element_type=F32)
        u = jnp.dot(xn, wu_ref[:, sl], preferred_element_type=F32)
        if row_scale is not None:
            g = g * row_scale
            u = u * row_scale
        a = (g * jax.nn.sigmoid(g) * u).astype(BF16)
        d = jnp.dot(a, wd_ref[sl, :], preferred_element_type=F32)
        acc = d if acc is None else acc + d
    return acc


def _rope_block(x, cos, sin_signed, first_half):
    partner = jnp.where(first_half, pltpu.roll(x, LANES - 32, 1), pltpu.roll(x, 32, 1))
    return x * cos + partner * sin_signed


def _ffn_in_kernel(x_ref, cos_ref, sin_ref, n1_ref, wg_ref, wu_ref, wd_ref, n2_ref, win_ref,
                   h_ref, qa_ref, ka_ref, va_ref, qb_ref, kb_ref, vb_ref, *, v_transposed):
    x = x_ref[...]
    xg, x_scale = _rms_split(x, n1_ref[...])
    h = x + 0.5 * _swiglu(xg, wg_ref, wu_ref, wd_ref, x_scale)
    h_ref[...] = h
    u, h_scale = _rms_split(h, n2_ref[...])

    rows = x.shape[0]
    cos = cos_ref[...]
    sin_signed = sin_ref[...]
    lane = lax.broadcasted_iota(jnp.int32, (rows, LANES), 1)
    first_half = (lane % HEAD_DIM) < (HEAD_DIM // 2)
    low = lane < HEAD_DIM

    def project(col0, width):
        return jnp.dot(u, win_ref[:, col0:col0 + width], preferred_element_type=F32) * h_scale

    def store_rope(dst_ref, z, scale):
        for i in range(z.shape[1] // LANES):
            blk = _rope_block(z[:, i * LANES:(i + 1) * LANES], cos, sin_signed, first_half)
            if scale != 1.0:
                blk = blk * scale
            dst_ref[:, i * LANES:(i + 1) * LANES] = blk.astype(dst_ref.dtype)

    def store_padded(dst_ref, z):
        swapped = pltpu.roll(z, HEAD_DIM, 1)
        zero = jnp.zeros_like(z)
        parts = (jnp.where(low, z, zero), jnp.where(low, zero, swapped),
                 jnp.where(low, swapped, zero), jnp.where(low, zero, z))
        for i, part in enumerate(parts):
            dst_ref[:, i * LANES:(i + 1) * LANES] = part.astype(dst_ref.dtype)

    def store_padded_transposed(dst_ref, z):
        zt = z.T
        swapped = jnp.concatenate([zt[HEAD_DIM:], zt[:HEAD_DIM]], axis=0)
        top = lax.broadcasted_iota(jnp.int32, zt.shape, 0) < HEAD_DIM
        zero = jnp.zeros_like(zt)
        parts = (jnp.where(top, zt, zero), jnp.where(top, zero, swapped),
                 jnp.where(top, swapped, zero), jnp.where(top, zero, zt))
        for i, part in enumerate(parts):
            dst_ref[i * LANES:(i + 1) * LANES, :] = part.astype(dst_ref.dtype)

    store_rope(qa_ref, project(0, 512), QK_SCALE * LOG2E)
    store_rope(ka_ref, project(512, 512), 1.0)
    store_rope(qb_ref, project(1536, 512), QK_SCALE * LOG2E)
    zkv = project(2048, 2 * LANES)
    store_padded(kb_ref, _rope_block(zkv[:, :LANES], cos, sin_signed, first_half))
    zva = project(1024, 512)
    zvb = zkv[:, LANES:]
    if v_transposed:
        for i in range(zva.shape[1] // LANES):
            va_ref[i * LANES:(i + 1) * LANES, :] = zva[:, i * LANES:(i + 1) * LANES].T.astype(va_ref.dtype)
        store_padded_transposed(vb_ref, zvb)
    else:
        va_ref[...] = zva.astype(va_ref.dtype)
        store_padded(vb_ref, zvb)


def _const_spec(shape):
    zeros = (0,) * len(shape)
    return pl.BlockSpec(shape, lambda *_: zeros, pipeline_mode=pl.Buffered(1))


def _ffn_in(x, cos, sin_signed, n1, wg, wu, wd, n2, win, row_tile, v_transposed):
    rows, d = x.shape
    pos_tiles = cos.shape[0] // row_tile
    row = lambda i: (i, 0)
    out_w = 512
    wide = pl.BlockSpec((row_tile, out_w), row)
    wide_shape = jax.ShapeDtypeStruct((rows, out_w), BF16)
    v_spec = pl.BlockSpec((out_w, row_tile), lambda i: (0, i)) if v_transposed else wide
    v_shape = jax.ShapeDtypeStruct((out_w, rows), BF16) if v_transposed else wide_shape
    return pl.pallas_call(
        functools.partial(_ffn_in_kernel, v_transposed=v_transposed),
        grid=(rows // row_tile,),
        in_specs=[
            pl.BlockSpec((row_tile, d), row),
            pl.BlockSpec((row_tile, LANES), lambda i: (i % pos_tiles, 0)),
            pl.BlockSpec((row_tile, LANES), lambda i: (i % pos_tiles, 0)),
            _const_spec(n1.shape), _const_spec(wg.shape), _const_spec(wu.shape), _const_spec(wd.shape),
            _const_spec(n2.shape), _const_spec(win.shape),
        ],
        out_specs=[pl.BlockSpec((row_tile, d), row), wide, wide, v_spec, wide, wide, v_spec],
        out_shape=[jax.ShapeDtypeStruct((rows, d), F32), wide_shape, wide_shape, v_shape, wide_shape, wide_shape,
                   v_shape],
        compiler_params=pltpu.CompilerParams(dimension_semantics=("parallel",),
                                             vmem_limit_bytes=VMEM_LIMIT_ROWWISE),
        name="ffn_in",
    )(x, cos, sin_signed, n1, wg, wu, wd, n2, win)


def _diff_attn_kernel(lq1_ref, lk1_ref, lq2_ref, lk2_ref, gain_ref, q_ref, km_ref, vmt_ref, k_ref, vt_ref,
                      o_ref, qz_sc, sa_sc, sb_sc, sm_sc, m_sc, acc_sc):
    lam = (jnp.exp(jnp.sum(lq1_ref[...] * lk1_ref[...], keepdims=True))
           - jnp.exp(jnp.sum(lq2_ref[...] * lk2_ref[...], keepdims=True)) + LAMBDA_INIT)
    tq, tk = DIFF_TQ, DIFF_TK
    vdim = vt_ref.shape[0]
    nq = q_ref.shape[0] // tq
    nk = vt_ref.shape[1] // tk

    def with_ones(vt):
        return jnp.concatenate([vt, jnp.ones((BF16_SUBLANES, vt.shape[1]), BF16)], axis=0)

    def load_q(i, slot):
        q = q_ref[pl.ds(pl.multiple_of(i * tq, tq), tq), :]
        lane = lax.broadcasted_iota(jnp.int32, q.shape, 1)
        zero = jnp.zeros_like(q)
        qz_sc[slot, 0] = jnp.where(lane < HEAD_DIM, q, zero)
        qz_sc[slot, 1] = jnp.where(lane < HEAD_DIM, zero, q)

    def produce(t, s_sc, slot):
        kt = k_ref[pl.ds(pl.multiple_of(t * tk, tk), tk), :]
        maxima = []
        for c in range(2):
            s = lax.dot_general(kt, qz_sc[slot, c], _NT, preferred_element_type=F32)
            s_sc[c] = s
            maxima.append(jnp.max(s, axis=0, keepdims=True))
        return tuple(maxima)

    def fold(c, s_ref, s_max, vt):
        m_prev = m_sc[c]
        m_new = jnp.maximum(m_prev, s_max)
        alpha = jnp.exp2(m_prev - m_new)
        m_sc[c] = m_new
        for n0 in range(0, tq, MXU_WIDTH):
            cols = slice(n0, n0 + MXU_WIDTH)
            p = jnp.exp2(s_ref[:, cols] - m_new[:, cols])
            acc_sc[c, :, cols] = (alpha[:, cols] * acc_sc[c, :, cols]
                                  + jnp.dot(vt, p.astype(BF16), preferred_element_type=F32))

    def step(t, s_sc, maxima, t_next, s_next_sc, slot_next):
        vt = with_ones(vt_ref[:, pl.ds(pl.multiple_of(t * tk, tk), tk)])
        stats = []
        for c in range(2):
            m_prev = m_sc[c]
            m_new = jnp.maximum(m_prev, maxima[c])
            stats.append((m_new, jnp.exp2(m_prev - m_new)))
            m_sc[c] = m_new
        kt_next = k_ref[pl.ds(pl.multiple_of(t_next * tk, tk), tk), :]
        next_max = [[], []]
        for c in range(2):
            m_new, alpha = stats[c]
            for n0 in range(0, tq, MXU_WIDTH):
                cols = slice(n0, n0 + MXU_WIDTH)
                s = lax.dot_general(kt_next, qz_sc[slot_next, c, cols, :], _NT, preferred_element_type=F32)
                s_next_sc[c, :, cols] = s
                next_max[c].append(jnp.max(s, axis=0, keepdims=True))
                p = jnp.exp2(s_sc[c, :, cols] - m_new[:, cols])
                acc_sc[c, :, cols] = (alpha[:, cols] * acc_sc[c, :, cols]
                                      + jnp.dot(vt, p.astype(BF16), preferred_element_type=F32))
        return tuple(jnp.concatenate(parts, axis=1) for parts in next_max)

    def meta_scores(slot):
        for c in range(2):
            sm_sc[c] = lax.dot_general(km_ref[...], qz_sc[slot, c], _NT, preferred_element_type=F32)

    def consume_meta():
        vmt = with_ones(vmt_ref[...])
        for c in range(2):
            fold(c, sm_sc.at[c], jnp.max(sm_sc[c], axis=0, keepdims=True), vmt)

    def finalize(i):
        def normalised(c):
            acc = acc_sc[c]
            return acc[:vdim] * (1.0 / acc[vdim:vdim + 1])

        a = normalised(0) - lam * normalised(1)
        ms = jnp.mean(a * a, axis=0, keepdims=True)
        y = a * lax.rsqrt(ms + EPS) * gain_ref[...] * (1.0 - LAMBDA_INIT)
        o_ref[pl.ds(pl.multiple_of(i * tq, tq), tq), :] = y.T.astype(o_ref.dtype)

    def query_tile(i, maxima_a):
        slot = i % 2
        for c in range(2):
            m_sc[c] = jnp.full(m_sc.shape[1:], NEG, F32)
            acc_sc[c] = jnp.zeros(acc_sc.shape[1:], F32)

        def pair(j, maxima_a):
            t = 2 * j
            maxima_b = step(t, sa_sc, maxima_a, t + 1, sb_sc, slot)
            return step(t + 1, sb_sc, maxima_b, t + 2, sa_sc, slot)

        maxima_a = lax.fori_loop(0, nk // 2 - 1, pair, maxima_a)
        meta_scores(slot)
        maxima_b = step(nk - 2, sa_sc, maxima_a, nk - 1, sb_sc, slot)
        load_q(jnp.minimum(i + 1, nq - 1), 1 - slot)
        maxima_next = step(nk - 1, sb_sc, maxima_b, 0, sa_sc, 1 - slot)
        consume_meta()
        finalize(i)
        return maxima_next

    load_q(0, 0)
    lax.fori_loop(0, nq, query_tile, produce(0, sa_sc, 0))


def _diff_attn(lams, gain_col, qa, ka_meta, vat_meta, ka, vat, batch):
    rows = qa.shape[0]
    seq = rows // batch
    vdim = 2 * HEAD_DIM
    tq, tk = DIFF_TQ, DIFF_TK
    assert (seq // tk) % 2 == 0
    lam_spec = pl.BlockSpec((1, HEAD_DIM), lambda b, h: (0, 0))
    return pl.pallas_call(
        _diff_attn_kernel,
        grid=(batch, DIFF_HEADS),
        in_specs=[
            lam_spec, lam_spec, lam_spec, lam_spec,
            pl.BlockSpec((vdim, 1), lambda b, h: (0, 0)),
            pl.BlockSpec((seq, LANES), lambda b, h: (b, h)),
            pl.BlockSpec((N_META, LANES), lambda b, h: (0, h)),
            pl.BlockSpec((vdim, N_META), lambda b, h: (h, 0)),
            pl.BlockSpec((seq, LANES), lambda b, h: (b, h)),
            pl.BlockSpec((vdim, seq), lambda b, h: (h, b)),
        ],
        out_specs=pl.BlockSpec((seq, LANES), lambda b, h: (b, h)),
        out_shape=jax.ShapeDtypeStruct((rows, DIFF_HEADS * vdim), BF16),
        scratch_shapes=[pltpu.VMEM((2, 2, tq, LANES), BF16),
                        pltpu.VMEM((2, tk, tq), F32), pltpu.VMEM((2, tk, tq), F32),
                        pltpu.VMEM((2, N_META, tq), F32),
                        pltpu.VMEM((2, 1, tq), F32),
                        pltpu.VMEM((2, vdim + BF16_SUBLANES, tq), F32)],
        compiler_params=pltpu.CompilerParams(dimension_semantics=("parallel", "parallel"),
                                             vmem_limit_bytes=VMEM_LIMIT_ATTN),
        name="diff_attn",
    )(*lams, gain_col, qa, ka_meta, vat_meta, ka, vat)


def _win_attn_kernel(sink_ref, q_ref, km_ref, vmt_ref, k_ref, vt_ref, o_ref,
                     delta_sc, sa_sc, sb_sc, sma_sc, smb_sc):
    g = pl.program_id(1)
    blk = pl.program_id(2)
    seq = k_ref.shape[0]
    tq, span = WIN_TQ, WIN_SPAN
    n = 2 * tq
    nt = q_ref.shape[0] // tq
    delta_sc[...] = (lax.broadcasted_iota(jnp.int32, (span, tq), 0)
                     - lax.broadcasted_iota(jnp.int32, (span, tq), 1))
    first_pair = lax.broadcasted_iota(jnp.int32, (1, n), 1) < tq
    sinks = [jnp.where(first_pair, sink_ref[g * WIN_GROUP + par], sink_ref[g * WIN_GROUP + 2 + par]) * LOG2E
             for par in range(2)]

    ones_row = (HEAD_DIM, 0)
    top_half = lax.broadcasted_iota(jnp.int32, (LANES, n), 0) < HEAD_DIM

    def key_start(i):
        t0 = blk * (nt * tq) + i * tq
        return t0, pl.multiple_of(jnp.clip(t0 - WINDOW, 0, seq - span), WINDOW)

    def produce_pieces(i, s_sc, sm_sc):
        t0, ks = key_start(i)
        off = ks - t0
        rows = pl.ds(pl.multiple_of(i * tq, tq), tq)

        def piece(par):
            delta = delta_sc[...]
            valid = (delta >= -WINDOW - off) & (delta <= WINDOW - off)
            q_cat = jnp.concatenate([q_ref[rows, :LANES], q_ref[rows, LANES:]], axis=0)
            part = slice(par * LANES, (par + 1) * LANES)
            k_cat = jnp.concatenate([k_ref[pl.ds(ks, span), part], km_ref[:, part]], axis=0)
            s_all = lax.dot_general(k_cat, q_cat, _NT, preferred_element_type=F32)
            s = s_all[:span]
            s = jnp.concatenate([jnp.where(valid, s[:, :tq], NEG), jnp.where(valid, s[:, tq:], NEG)], axis=1)
            sm = s_all[span:]
            s_sc[par] = s
            sm_sc[par] = sm
            return jnp.maximum(jnp.maximum(jnp.max(s, axis=0, keepdims=True),
                                           jnp.max(sm, axis=0, keepdims=True)), sinks[par])

        return [functools.partial(piece, par) for par in range(2)]

    def consume_pieces(i, s_sc, sm_sc, maxima):
        _, ks = key_start(i)

        def with_ones(vt, par):
            r = ones_row[par]
            ones = jnp.ones((BF16_SUBLANES, vt.shape[1]), BF16)
            return jnp.concatenate(([vt[:r]] if r else []) + [ones, vt[r + BF16_SUBLANES:]], axis=0)

        def piece(par):
            part = slice(par * LANES, (par + 1) * LANES)
            m = maxima[par]
            p = jnp.exp2(s_sc[par] - m)
            pm = jnp.exp2(sm_sc[par] - m)
            o = (jnp.dot(with_ones(vt_ref[part, pl.ds(ks, span)], par), p.astype(BF16),
                         preferred_element_type=F32)
                 + jnp.dot(with_ones(vmt_ref[part, :], par), pm.astype(BF16), preferred_element_type=F32))
            l = o[ones_row[par]:ones_row[par] + 1] + jnp.exp2(sinks[par] - m)
            return o * (1.0 / l)

        def store(outs):
            o_t = jnp.where(top_half, outs[0], outs[1])
            rows = pl.ds(pl.multiple_of(i * tq, tq), tq)
            for pair in range(WIN_GROUP // 2):
                o_ref[rows, pair * LANES:(pair + 1) * LANES] = (
                    o_t[:, pair * tq:(pair + 1) * tq].T.astype(o_ref.dtype))

        return [functools.partial(piece, par) for par in range(2)], store

    def produce(i, s_sc, sm_sc):
        return tuple(piece() for piece in produce_pieces(i, s_sc, sm_sc))

    def consume(i, s_sc, sm_sc, maxima):
        pieces, store = consume_pieces(i, s_sc, sm_sc, maxima)
        store([piece() for piece in pieces])

    def step(i, s_sc, sm_sc, maxima, i_next, s_next_sc, sm_next_sc):
        makers = produce_pieces(i_next, s_next_sc, sm_next_sc)
        users, store = consume_pieces(i, s_sc, sm_sc, maxima)
        next_maxima, outs = [], []
        for make, use in zip(makers, users):
            next_maxima.append(make())
            outs.append(use())
        store(outs)
        return tuple(next_maxima)

    def pair_of_tiles(j, maxima_a):
        i = 2 * j
        maxima_b = step(i, sa_sc, sma_sc, maxima_a, i + 1, sb_sc, smb_sc)
        return step(i + 1, sb_sc, smb_sc, maxima_b, i + 2, sa_sc, sma_sc)

    maxima_a = lax.fori_loop(0, nt // 2 - 1, pair_of_tiles, produce(0, sa_sc, sma_sc), unroll=5)
    maxima_b = step(nt - 2, sa_sc, sma_sc, maxima_a, nt - 1, sb_sc, smb_sc)
    consume(nt - 1, sb_sc, smb_sc, maxima_b)


def _win_attn(sink, qb, kb_meta, vbt_meta, kb, vbt, batch):
    rows = qb.shape[0]
    seq = rows // batch
    nblk = seq // WIN_QBLOCK
    assert (WIN_QBLOCK // WIN_TQ) % 2 == 0
    width = 2 * LANES
    n = 2 * WIN_TQ
    return pl.pallas_call(
        _win_attn_kernel,
        grid=(batch, WIN_KV_HEADS, nblk),
        in_specs=[
            pl.BlockSpec(memory_space=pltpu.SMEM),
            pl.BlockSpec((WIN_QBLOCK, width), lambda b, g, i: (b * nblk + i, g)),
            pl.BlockSpec((N_META, width), lambda b, g, i: (0, g)),
            pl.BlockSpec((width, N_META), lambda b, g, i: (g, 0)),
            pl.BlockSpec((seq, width), lambda b, g, i: (b, g)),
            pl.BlockSpec((width, seq), lambda b, g, i: (g, b)),
        ],
        out_specs=pl.BlockSpec((WIN_QBLOCK, width), lambda b, g, i: (b * nblk + i, g)),
        out_shape=jax.ShapeDtypeStruct((rows, WIN_KV_HEADS * width), BF16),
        scratch_shapes=[pltpu.VMEM((WIN_SPAN, WIN_TQ), jnp.int32),
                        pltpu.VMEM((2, WIN_SPAN, n), F32), pltpu.VMEM((2, WIN_SPAN, n), F32),
                        pltpu.VMEM((2, N_META, n), F32), pltpu.VMEM((2, N_META, n), F32)],
        compiler_params=pltpu.CompilerParams(dimension_semantics=("parallel", "parallel", "arbitrary"),
                                             vmem_limit_bytes=VMEM_LIMIT_ATTN),
        name="win_attn",
    )(sink, qb, kb_meta, vbt_meta, kb, vbt)


def _out_ffn_kernel(h_ref, oa_ref, ob_ref, wn_ref, wo_ref, n_ref, wg_ref, wu_ref, wd_ref, fn_ref, out_ref):
    half = oa_ref.shape[1]
    ob, ob_scale = _rms_split(ob_ref[...].astype(F32), wn_ref[...])
    mix = (jnp.dot(oa_ref[...], wo_ref[:half, :], preferred_element_type=F32)
           + jnp.dot(ob, wo_ref[half:, :], preferred_element_type=F32) * ob_scale)
    h = h_ref[...] + mix
    hg, h_scale = _rms_split(h, n_ref[...])
    h = h + 0.5 * _swiglu(hg, wg_ref, wu_ref, wd_ref, h_scale)
    out_ref[...] = _rms(h, fn_ref[...])


def _out_ffn(h1, oa, ob, wn, wo, n, wg, wu, wd, fn):
    rows, d = h1.shape
    half = oa.shape[1]
    row = lambda i: (i, 0)
    return pl.pallas_call(
        _out_ffn_kernel,
        grid=(rows // OUT_ROW_TILE,),
        in_specs=[
            pl.BlockSpec((OUT_ROW_TILE, d), row), pl.BlockSpec((OUT_ROW_TILE, half), row),
            pl.BlockSpec((OUT_ROW_TILE, half), row),
            _const_spec(wn.shape), _const_spec(wo.shape), _const_spec(n.shape),
            _const_spec(wg.shape), _const_spec(wu.shape), _const_spec(wd.shape), _const_spec(fn.shape),
        ],
        out_specs=pl.BlockSpec((OUT_ROW_TILE, d), row),
        out_shape=jax.ShapeDtypeStruct((rows, d), F32),
        compiler_params=pltpu.CompilerParams(dimension_semantics=("parallel",),
                                             vmem_limit_bytes=VMEM_LIMIT_ROWWISE),
        name="out_ffn",
    )(h1, oa, ob, wn, wo, n, wg, wu, wd, fn)


def _rope_tables(length):
    pos = jnp.arange(length, dtype=F32)
    inv = ROPE_THETA ** (-jnp.arange(0, HEAD_DIM, 2, dtype=F32) / HEAD_DIM)
    ang = pos[:, None] * inv[None, :]
    cos, sin = jnp.cos(ang), jnp.sin(ang)
    return jnp.tile(cos, (1, 4)), jnp.tile(jnp.concatenate([-sin, sin], axis=1), (1, 2))


def kernel(x, meta_tokens, ffn1_norm, ffn1_w_gate, ffn1_w_up, ffn1_w_down, mix_norm, w_in, lambda_q1, lambda_k1, lambda_q2, lambda_k2, diff_norm, win_sink, win_norm, w_out, ffn2_norm, ffn2_w_gate, ffn2_w_up, ffn2_w_down, final_norm):
    batch, seq, d = x.shape
    assert ffn1_norm.shape[0] == 1, "single layer only"
    assert seq % ROW_TILE == 0 and seq % DIFF_TQ == 0 and seq % DIFF_TK == 0 and seq % WIN_QBLOCK == 0

    cos, sin_signed = _rope_tables(N_META + seq)
    ffn_in_weights = (ffn1_norm, ffn1_w_gate[0].astype(BF16), ffn1_w_up[0].astype(BF16),
                      ffn1_w_down[0].astype(BF16), mix_norm, w_in[0].astype(BF16))

    meta = _ffn_in(meta_tokens.astype(x.dtype), cos[:N_META], sin_signed[:N_META], *ffn_in_weights, N_META,
                   v_transposed=False)
    real = _ffn_in(x.reshape(batch * seq, d), cos[N_META:], sin_signed[N_META:], *ffn_in_weights, ROW_TILE,
                   v_transposed=True)
    _, _, ka_m, va_m, _, kb_m, vb_m = meta
    h1, qa, ka, vat, qb, kb, vbt = real

    lams = (lambda_q1, lambda_k1, lambda_q2, lambda_k2)
    out_a = _diff_attn(lams, diff_norm.reshape(2 * HEAD_DIM, 1), qa, ka_m, va_m.T, ka, vat, batch)

    out_b = _win_attn(win_sink.reshape(-1), qb, kb_m, vb_m.T, kb, vbt, batch)

    out = _out_ffn(h1, out_a, out_b, win_norm, w_out[0].astype(BF16), ffn2_norm,
                   ffn2_w_gate[0].astype(BF16), ffn2_w_up[0].astype(BF16), ffn2_w_down[0].astype(BF16),
                   final_norm.reshape(1, d))
    return out.reshape(batch, seq, d)
```
